```python
import jax, jax.numpy as jnp
from jax import lax
import numpy as np

D_MODEL = 2048
BATCH = 4
SEQ = 2048
DEPTH = 4
DEC_BATCH = 1
DEC_SEQ = 8192
PAST_LEN = 128

N_MIXERS = 3
N_LAYERS_A = (DEPTH + 2) // 3
N_LAYERS_B = (DEPTH + 1) // 3
N_LAYERS_C = DEPTH // 3
EPS = 1e-6
Q_BLOCK = 128
NEG = -1e30

A_HEADS = 16
A_Q_RANK = 512
A_KV_RANK = 512
A_NOPE = 128
A_ROPE = 64
A_V = 128
A_THETA = 10000.0
A_WIDTH = A_HEADS * A_V
A_IN = A_Q_RANK + A_KV_RANK + A_ROPE + A_WIDTH

B_HEADS = 16
B_KV_HEADS = 4
B_HEAD_DIM = 128
B_GROUP = B_HEADS // B_KV_HEADS
B_WINDOW = 128
B_ROT = B_HEAD_DIM // 4
ROPE_THETA = 500000.0
B_WIDTH = B_HEADS * B_HEAD_DIM
B_KV_WIDTH = B_KV_HEADS * B_HEAD_DIM
B_IN = B_WIDTH + 2 * B_KV_WIDTH + B_WIDTH

C_HEADS = 16
C_KV_HEADS = 4
C_HEAD_DIM = 128
C_GROUP = C_HEADS // C_KV_HEADS
C_THETA = 10000.0
GRID_W = 64
C_WIDTH = C_HEADS * C_HEAD_DIM
C_KV_WIDTH = C_KV_HEADS * C_HEAD_DIM
C_IN = C_WIDTH + 2 * C_KV_WIDTH + C_WIDTH

kernel_name = "hybrid_mla_swa_axial_encoder"


def rms_norm(x, g):
    xf = x.astype(jnp.float32)
    y = xf * lax.rsqrt(jnp.mean(xf * xf, axis=-1, keepdims=True) + EPS)
    return (y * g.astype(jnp.float32)).astype(x.dtype)


def rope_angles(pos, dim, theta):
    inv = theta ** (-jnp.arange(dim // 2, dtype=jnp.float32) * (2.0 / dim))
    ang = pos.astype(jnp.float32)[:, None] * inv[None, :]
    return jnp.cos(ang), jnp.sin(ang)


def apply_rope(x, cos, sin):
    d2 = x.shape[-1] // 2
    xf = x.astype(jnp.float32)
    x1, x2 = xf[..., :d2], xf[..., d2:]
    c = cos[None, :, None, :]
    s = sin[None, :, None, :]
    return jnp.concatenate([x1 * c - x2 * s, x2 * c + x1 * s], axis=-1).astype(x.dtype)


def dense_attention_blocks(q, k, v):
    b, s, hkv, g, dk = q.shape
    nb = s // Q_BLOCK
    scale = dk ** -0.5
    qb = jnp.moveaxis(q.reshape(b, nb, Q_BLOCK, hkv, g, dk), 1, 0)

    def one_block(qi):
        sc = jnp.einsum('bqhgd,bshd->bhgqs', qi, k, preferred_element_type=jnp.float32) * scale
        p = jax.nn.softmax(sc, axis=-1).astype(v.dtype)
        return jnp.einsum('bhgqs,bshd->bqhgd', p, v)

    out = lax.map(one_block, qb)
    return jnp.moveaxis(out, 0, 1).reshape(b, s, hkv, g, v.shape[-1])


def banded_sink_attention(q, k, v, sink):
    b, s, hkv, g, d = q.shape
    nb = s // Q_BLOCK
    qb = q.reshape(b, nb, Q_BLOCK, hkv, g, d)
    pad = ((0, 0), (Q_BLOCK, Q_BLOCK), (0, 0), (0, 0))
    kp = jnp.pad(k, pad).reshape(b, nb + 2, Q_BLOCK, hkv, d)
    vp = jnp.pad(v, pad).reshape(b, nb + 2, Q_BLOCK, hkv, d)
    kb = jnp.concatenate([kp[:, :-2], kp[:, 1:-1], kp[:, 2:]], axis=2)
    vb = jnp.concatenate([vp[:, :-2], vp[:, 1:-1], vp[:, 2:]], axis=2)
    qpos = jnp.arange(s).reshape(nb, Q_BLOCK)
    kpos = jnp.arange(nb)[:, None] * Q_BLOCK - Q_BLOCK + jnp.arange(3 * Q_BLOCK)[None, :]
    rel = kpos[:, None, :] - qpos[:, :, None]
    valid = (jnp.abs(rel) <= B_WINDOW) & (kpos[:, None, :] >= 0) & (kpos[:, None, :] < s)
    sc = jnp.einsum('bnqhgd,bnkhd->bnhgqk', qb, kb, preferred_element_type=jnp.float32) * (d ** -0.5)
    sc = jnp.where(valid[None, :, None, None, :, :], sc, NEG)
    sink_col = jnp.broadcast_to(sink.astype(jnp.float32).reshape(hkv, g)[None, None, :, :, None, None],
                                sc.shape[:-1] + (1,))
    p = jax.nn.softmax(jnp.concatenate([sc, sink_col], axis=-1), axis=-1)[..., :-1]
    out = jnp.einsum('bnhgqk,bnkhd->bnqhgd', p.astype(v.dtype), vb)
    return out.reshape(b, s, hkv, g, d)


def mla_branch(h, w_in, q_norm, w_uq, kv_norm, w_ukv, w_o):
    b, s, _ = h.shape
    proj = h @ w_in
    c_q, c_kv, k_rope, gate = jnp.split(
        proj, [A_Q_RANK, A_Q_RANK + A_KV_RANK, A_Q_RANK + A_KV_RANK + A_ROPE], axis=-1)
    q = (rms_norm(c_q, q_norm) @ w_uq).reshape(b, s, A_HEADS, A_NOPE + A_ROPE)
    kv = (rms_norm(c_kv, kv_norm) @ w_ukv).reshape(b, s, A_HEADS, A_NOPE + A_V)
    cos, sin = rope_angles(jnp.arange(s), A_ROPE, A_THETA)
    q = jnp.concatenate([q[..., :A_NOPE], apply_rope(q[..., A_NOPE:], cos, sin)], axis=-1)
    k_r = apply_rope(k_rope[:, :, None, :], cos, sin)
    k = jnp.concatenate([kv[..., :A_NOPE], jnp.broadcast_to(k_r, (b, s, A_HEADS, A_ROPE))], axis=-1)
    v = kv[..., A_NOPE:]
    o = dense_attention_blocks(q[:, :, :, None, :], k, v).reshape(b, s, A_WIDTH)
    return (o * jax.nn.silu(gate)) @ w_o


def swa_branch(h, w_in, sink, w_o):
    b, s, _ = h.shape
    proj = h @ w_in
    q, k, v, gate = jnp.split(proj, [B_WIDTH, B_WIDTH + B_KV_WIDTH, B_WIDTH + 2 * B_KV_WIDTH], axis=-1)
    q = q.reshape(b, s, B_HEADS, B_HEAD_DIM)
    k = k.reshape(b, s, B_KV_HEADS, B_HEAD_DIM)
    v = v.reshape(b, s, B_KV_HEADS, B_HEAD_DIM)
    cos, sin = rope_angles(jnp.arange(s), B_ROT, ROPE_THETA)
    q = jnp.concatenate([apply_rope(q[..., :B_ROT], cos, sin), q[..., B_ROT:]], axis=-1)
    k = jnp.concatenate([apply_rope(k[..., :B_ROT], cos, sin), k[..., B_ROT:]], axis=-1)
    o = banded_sink_attention(q.reshape(b, s, B_KV_HEADS, B_GROUP, B_HEAD_DIM), k, v, sink)
    return (o.reshape(b, s, B_WIDTH) * jax.nn.silu(gate)) @ w_o


def axial_branch(h, w_in, q_scale, k_scale, w_o):
    b, s, _ = h.shape
    proj = h @ w_in
    q, k, v, gate = jnp.split(proj, [C_WIDTH, C_WIDTH + C_KV_WIDTH, C_WIDTH + 2 * C_KV_WIDTH], axis=-1)
    q = rms_norm(q.reshape(b, s, C_HEADS, C_HEAD_DIM), q_scale)
    k = rms_norm(k.reshape(b, s, C_KV_HEADS, C_HEAD_DIM), k_scale)
    v = v.reshape(b, s, C_KV_HEADS, C_HEAD_DIM)
    rows = s // GRID_W
    row = jnp.repeat(jnp.arange(rows), GRID_W)
    col = jnp.tile(jnp.arange(GRID_W), rows)
    half = C_HEAD_DIM // 2
    cr, sr = rope_angles(row, half, C_THETA)
    cc, scol = rope_angles(col, half, C_THETA)
    q = jnp.concatenate([apply_rope(q[..., :half], cr, sr), apply_rope(q[..., half:], cc, scol)], axis=-1)
    k = jnp.concatenate([apply_rope(k[..., :half], cr, sr), apply_rope(k[..., half:], cc, scol)], axis=-1)
    o = dense_attention_blocks(q.reshape(b, s, C_KV_HEADS, C_GROUP, C_HEAD_DIM), k, v)
    return (o.reshape(b, s, C_WIDTH) * jax.nn.silu(gate)) @ w_o


def trunk(x, norm_w, a_w_in, a_q_norm, a_w_uq, a_kv_norm, a_w_ukv, a_w_o,
          b_w_in, b_sink, b_w_o, c_w_in, c_q_scale, c_k_scale, c_w_o, final_norm):
    for i in range(DEPTH):
        kind, j = i % N_MIXERS, i // N_MIXERS
        hn = rms_norm(x, norm_w[i])
        if kind == 0:
            x = x + mla_branch(hn, a_w_in[j], a_q_norm[j], a_w_uq[j], a_kv_norm[j], a_w_ukv[j], a_w_o[j])
        elif kind == 1:
            x = x + swa_branch(hn, b_w_in[j], b_sink[j], b_w_o[j])
        else:
            x = x + axial_branch(hn, c_w_in[j], c_q_scale[j], c_k_scale[j], c_w_o[j])
    return rms_norm(x, final_norm)


def setup_inputs(seed: int = 0) -> dict:
    key = jax.random.key(seed)
    ks = jax.random.split(key, 20)

    def w(k, shape, fan_in):
        return jax.random.normal(k, shape, jnp.float32) * (fan_in ** -0.5)

    def gain(k, shape):
        return 1.0 + 0.05 * jax.random.normal(k, shape, jnp.float32)

    return {
        "x_prompt": jax.random.normal(ks[0], (BATCH, SEQ, D_MODEL), jnp.float32),
        "x_sample": jax.random.normal(ks[1], (DEC_BATCH, DEC_SEQ, D_MODEL), jnp.float32),
        "norm_w": gain(ks[2], (DEPTH, D_MODEL)),
        "a_w_in": w(ks[3], (N_LAYERS_A, D_MODEL, A_IN), D_MODEL),
        "a_q_norm": gain(ks[4], (N_LAYERS_A, A_Q_RANK)),
        "a_w_uq": w(ks[5], (N_LAYERS_A, A_Q_RANK, A_HEADS * (A_NOPE + A_ROPE)), A_Q_RANK),
        "a_kv_norm": gain(ks[6], (N_LAYERS_A, A_KV_RANK)),
        "a_w_ukv": w(ks[7], (N_LAYERS_A, A_KV_RANK, A_HEADS * (A_NOPE + A_V)), A_KV_RANK),
        "a_w_o": w(ks[8], (N_LAYERS_A, A_WIDTH, D_MODEL), A_WIDTH),
        "b_w_in": w(ks[9], (N_LAYERS_B, D_MODEL, B_IN), D_MODEL),
        "b_sink": 0.5 * jax.random.normal(ks[10], (N_LAYERS_B, B_HEADS), jnp.float32),
        "b_w_o": w(ks[11], (N_LAYERS_B, B_WIDTH, D_MODEL), B_WIDTH),
        "c_w_in": w(ks[12], (N_LAYERS_C, D_MODEL, C_IN), D_MODEL),
        "c_q_scale": gain(ks[13], (N_LAYERS_C, C_HEAD_DIM)),
        "c_k_scale": gain(ks[14], (N_LAYERS_C, C_HEAD_DIM)),
        "c_w_o": w(ks[15], (N_LAYERS_C, C_WIDTH, D_MODEL), C_WIDTH),
        "final_norm": gain(ks[16], (D_MODEL,)),
    }


def reference(x_prompt, x_sample, norm_w, a_w_in, a_q_norm, a_w_uq, a_kv_norm, a_w_ukv, a_w_o,
              b_w_in, b_sink, b_w_o, c_w_in, c_q_scale, c_k_scale, c_w_o, final_norm):
    y_prompt = trunk(x_prompt, norm_w, a_w_in, a_q_norm, a_w_uq, a_kv_norm, a_w_ukv, a_w_o,
                     b_w_in, b_sink, b_w_o, c_w_in, c_q_scale, c_k_scale, c_w_o, final_norm)
    y_sample = trunk(x_sample, norm_w, a_w_in, a_q_norm, a_w_uq, a_kv_norm, a_w_ukv, a_w_o,
                     b_w_in, b_sink, b_w_o, c_w_in, c_q_scale, c_k_scale, c_w_o, final_norm)
    return (y_prompt, y_sample)
```

```python
import dataclasses
import functools
import math

import jax
import jax.numpy as jnp
from jax import lax
from jax.experimental import pallas as pl
from jax.experimental.pallas import tpu as pltpu

F32 = jnp.float32
BF16 = jnp.bfloat16

LANES = 128
V7X_VMEM_BYTES = 64 * 1024 * 1024
VMEM_CEILING = V7X_VMEM_BYTES - 8 * 1024 * 1024

EPS = 1e-6
NEG = -1e30
LOG2E = math.log2(math.e)
A_HEADS, A_Q_RANK, A_KV_RANK, A_NOPE, A_ROPE, A_V = 16, 512, 512, 128, 64, 128
A_THETA = 10000.0
B_HEADS, B_KV_HEADS, B_HEAD_DIM, B_WINDOW, B_ROT = 16, 4, 128, 128, 32
B_THETA = 500000.0
C_HEADS, C_KV_HEADS, C_HEAD_DIM, GRID_W = 16, 4, 128, 64
C_THETA = 10000.0

DOT_COLS = 512


def _vmem_limit(estimate_bytes):
    return int(min(max(estimate_bytes, 16 * 1024 * 1024), VMEM_CEILING))


@dataclasses.dataclass(frozen=True)
class Seg:
    kind: str
    w_col: int
    width: int
    out_idx: int
    out_col: int
    scale: float = 1.0
    gain_idx: int = -1


def _proj_kernel(*refs, segs, n_tables, n_gains, n_out, roll_by):
    x_ref, gamma_ref, w_ref = refs[:3]
    pos = 3
    tables = refs[pos:pos + n_tables]
    pos += n_tables
    gains = refs[pos:pos + n_gains]
    pos += n_gains
    outs = refs[pos:pos + n_out]
    xn_ref = refs[pos + n_out]

    x = x_ref[...].astype(F32)
    ms = jnp.mean(x * x, axis=-1, keepdims=True)
    xn_ref[...] = (x * lax.rsqrt(ms + EPS) * gamma_ref[...]).astype(BF16)

    for seg in segs:
        out_ref = outs[seg.out_idx]
        for c0 in range(0, seg.width, DOT_COLS):
            w = min(DOT_COLS, seg.width - c0)
            acc = jnp.dot(xn_ref[...], w_ref[:, seg.w_col + c0:seg.w_col + c0 + w],
                          preferred_element_type=F32)
            if seg.kind == "plain":
                if seg.scale != 1.0:
                    acc = acc * seg.scale
                out_ref[:, seg.out_col + c0:seg.out_col + c0 + w] = acc.astype(out_ref.dtype)
                continue
            cos_ref, s1_ref, s2_ref = tables
            for l0 in range(0, w, LANES):
                a = acc[:, l0:l0 + LANES]
                if seg.kind == "hnorm_rope":
                    hms = jnp.mean(a * a, axis=-1, keepdims=True)
                    a = a * lax.rsqrt(hms + EPS) * gains[seg.gain_idx][...]
                a = (a * cos_ref[...] + pltpu.roll(a, roll_by, 1) * s1_ref[...]
                     + pltpu.roll(a, LANES - roll_by, 1) * s2_ref[...])
                if seg.scale != 1.0:
                    a = a * seg.scale
                o0 = seg.out_col + c0 + l0
                out_ref[:, o0:o0 + LANES] = a.astype(out_ref.dtype)


def _proj_tile(n_tok, k_dim, w_cols, out_cols, x_bytes):
    for tm in (512, 256, 128):
        if n_tok % tm:
            continue
        est = (k_dim * w_cols * 2 + 2 * tm * k_dim * x_bytes + 2 * tm * out_cols * 2
               + tm * k_dim * 2 + 3 * tm * DOT_COLS * 4 + 6 * tm * LANES * 4)
        if est <= VMEM_CEILING - 4 * 1024 * 1024:
            return tm, est
    raise ValueError("projection does not fit VMEM")


def _projection(x, x_col_block, k_dim, gamma, w, segs, out_widths, seq_len, tables=(), gains=(),
                roll_by=0, name="proj"):
    n_tok = x.shape[0]
    w_cols = w.shape[1]
    tm, est = _proj_tile(n_tok, k_dim, w_cols, sum(out_widths), x.dtype.itemsize)
    tm = min(tm, seq_len)
    pos_blocks = seq_len // tm
    in_specs = [
        pl.BlockSpec((tm, k_dim), lambda i: (i, x_col_block)),
        pl.BlockSpec((1, k_dim), lambda i: (0, 0)),
        pl.BlockSpec((k_dim, w_cols), lambda i: (0, 0), pipeline_mode=pl.Buffered(1)),
    ]
    in_specs += [pl.BlockSpec((tm, LANES), lambda i: (i % pos_blocks, 0)) for _ in tables]
    in_specs += [pl.BlockSpec((1, LANES), lambda i: (0, 0)) for _ in gains]
    out_specs = [pl.BlockSpec((tm, ow), lambda i: (i, 0)) for ow in out_widths]
    out_shape = [jax.ShapeDtypeStruct((n_tok, ow), BF16) for ow in out_widths]
    kern = functools.partial(_proj_kernel, segs=tuple(segs), n_tables=len(tables), n_gains=len(gains),
                             n_out=len(out_widths), roll_by=roll_by)
    return pl.pallas_call(
        kern,
        grid=(n_tok // tm,),
        in_specs=in_specs,
        out_specs=out_specs,
        out_shape=out_shape,
        scratch_shapes=[pltpu.VMEM((tm, k_dim), BF16)],
        compiler_params=pltpu.CompilerParams(dimension_semantics=("parallel",),
                                             vmem_limit_bytes=_vmem_limit(est)),
        name=name,
    )(x, gamma.reshape(1, k_dim).astype(F32), w, *tables, *gains)


def _dense_attn_kernel(*refs, n_q, n_k, group, tk, seq_len):
    q_refs = refs[:n_q]
    k_refs = refs[n_q:n_q + n_k]
    v_ref = refs[n_q + n_k]
    o_ref = refs[n_q + n_k + 1]
    tq = o_ref.shape[0]
    rows = group * tq

    def stack_heads(ref):
        return jnp.concatenate([ref[:, g * LANES:(g + 1) * LANES] for g in range(group)], axis=0)

    q = jnp.concatenate([stack_heads(r) for r in q_refs], axis=1)

    def body(j, carry):
        m, l, acc = carry
        start = pl.multiple_of(j * tk, tk)
        k = jnp.concatenate([r[pl.ds(start, tk), :] for r in k_refs], axis=1)
        v = v_ref[pl.ds(start, tk), :]
        s = lax.dot_general(q, k, (((1,), (1,)), ((), ())), preferred_element_type=F32)
        m_new = jnp.maximum(m, jnp.max(s, axis=-1, keepdims=True))
        alpha = jnp.exp2(m - m_new)
        p = jnp.exp2(s - m_new)
        l = alpha * l + jnp.sum(p, axis=-1, keepdims=True)
        acc = alpha * acc + jnp.dot(p.astype(BF16), v, preferred_element_type=F32)
        return m_new, l, acc

    init = (jnp.full((rows, 1), -jnp.inf, F32), jnp.zeros((rows, 1), F32),
            jnp.zeros((rows, LANES), F32))
    _, l, acc = lax.fori_loop(0, seq_len // tk, body, init)
    out = acc / l
    for g in range(group):
        o_ref[:, g * LANES:(g + 1) * LANES] = out[g * tq:(g + 1) * tq, :].astype(o_ref.dtype)


def _dense_attention(q_list, k_list, v, n_seq, seq_len, n_kv_heads, group, name):
    n_tok = n_seq * seq_len
    tq = min(512 // group, seq_len)
    tk = min(512, seq_len)
    q_blocks = seq_len // tq
    gw = group * LANES

    q_spec = pl.BlockSpec((tq, gw), lambda b, h, i: (b * q_blocks + i, h))
    in_specs = [q_spec for _ in q_list]
    for k in k_list:
        if k.shape[1] == LANES:
            in_specs.append(pl.BlockSpec((seq_len, LANES), lambda b, h, i: (b, 0)))
        else:
            in_specs.append(pl.BlockSpec((seq_len, LANES), lambda b, h, i: (b, h)))
    in_specs.append(pl.BlockSpec((seq_len, LANES), lambda b, h, i: (b, h)))
    rows = group * tq
    est = (2 * len(q_list) * tq * gw * 2 + 2 * (len(k_list) + 1) * seq_len * LANES * 2 + 2 * tq * gw * 2
           + 4 * rows * tk * 4 + 4 * rows * LANES * 4)
    kern = functools.partial(_dense_attn_kernel, n_q=len(q_list), n_k=len(k_list), group=group, tk=tk,
                             seq_len=seq_len)
    return pl.pallas_call(
        kern,
        grid=(n_seq, n_kv_heads, q_blocks),
        in_specs=in_specs,
        out_specs=q_spec,
        out_shape=jax.ShapeDtypeStruct((n_tok, n_kv_heads * gw), BF16),
        compiler_params=pltpu.CompilerParams(dimension_semantics=("parallel", "parallel", "arbitrary"),
                                             vmem_limit_bytes=_vmem_limit(est)),
        name=name,
    )(*q_list, *k_list, v)


def _banded_attn_kernel(sink_ref, q_ref, k_ref, v_ref, o_ref, *, group, qb, window, seq_len):
    tq = o_ref.shape[0]
    n_sub = tq // qb
    span = qb + 2 * window
    h = pl.program_id(1)
    i = pl.program_id(2)
    rows = group * qb

    sink_col = jnp.concatenate(
        [jnp.full((qb, 1), sink_ref[h * group + g] * LOG2E, F32) for g in range(group)], axis=0)
    rel0 = (lax.broadcasted_iota(jnp.int32, (qb, span), 1) - lax.broadcasted_iota(jnp.int32, (qb, span), 0))

    def body(sb, carry):
        q0 = pl.multiple_of(sb * qb, qb)
        q_abs = i * tq + q0
        k0 = jnp.clip(q_abs - window, 0, seq_len - span)
        k0 = pl.multiple_of(k0, LANES)
        rel = rel0 + (k0 - q_abs)
        bias = jnp.where(jnp.abs(rel) <= window, 0.0, NEG).astype(F32)
        q = jnp.concatenate([q_ref[pl.ds(q0, qb), g * LANES:(g + 1) * LANES] for g in range(group)], axis=0)
        k = k_ref[pl.ds(k0, span), :]
        v = v_ref[pl.ds(k0, span), :]
        s = lax.dot_general(q, k, (((1,), (1,)), ((), ())), preferred_element_type=F32)
        s = s + jnp.concatenate([bias] * group, axis=0)
        m = jnp.maximum(jnp.max(s, axis=-1, keepdims=True), sink_col)
        p = jnp.exp2(s - m)
        denom = jnp.sum(p, axis=-1, keepdims=True) + jnp.exp2(sink_col - m)
        out = jnp.dot(p.astype(BF16), v, preferred_element_type=F32) / denom
        for g in range(group):
            o_ref[pl.ds(q0, qb), g * LANES:(g + 1) * LANES] = out[g * qb:(g + 1) * qb, :].astype(o_ref.dtype)
        return carry

    lax.fori_loop(0, n_sub, body, 0)
    del rows


def _banded_attention(q, k, v, sink, n_seq, seq_len, name):
    n_tok = n_seq * seq_len
    group = B_HEADS // B_KV_HEADS
    gw = group * LANES
    qb, window = 128, B_WINDOW
    tq = min(1024, seq_len)
    q_blocks = seq_len // tq
    q_spec = pl.BlockSpec((tq, gw), lambda b, h, i: (b * q_blocks + i, h))
    kv_spec = pl.BlockSpec((seq_len, LANES), lambda b, h, i: (b, h))
    span = qb + 2 * window
    est = 4 * tq * gw * 2 + 4 * seq_len * LANES * 2 + 6 * group * qb * span * 4
    kern = functools.partial(_banded_attn_kernel, group=group, qb=qb, window=window, seq_len=seq_len)
    return pl.pallas_call(
        kern,
        grid=(n_seq, B_KV_HEADS, q_blocks),
        in_specs=[pl.BlockSpec(memory_space=pltpu.SMEM), q_spec, kv_spec, kv_spec],
        out_specs=q_spec,
        out_shape=jax.ShapeDtypeStruct((n_tok, B_HEADS * LANES), BF16),
        compiler_params=pltpu.CompilerParams(dimension_semantics=("parallel", "parallel", "arbitrary"),
                                             vmem_limit_bytes=_vmem_limit(est)),
        name=name,
    )(sink.astype(F32), q, k, v)


def _out_kernel(*refs, final):
    if final:
        x_ref, o_ref, g_ref, w_ref, fn_ref, y_ref = refs
    else:
        x_ref, o_ref, g_ref, w_ref, y_ref = refs
    g = g_ref[...].astype(F32)
    u = (o_ref[...].astype(F32) * (g / (1.0 + jnp.exp(-g)))).astype(BF16)
    y = x_ref[...] + jnp.dot(u, w_ref[...], preferred_element_type=F32)
    if final:
        ms = jnp.mean(y * y, axis=-1, keepdims=True)
        y = y * lax.rsqrt(ms + EPS) * fn_ref[...]
    y_ref[...] = y


def _output(x, o, gate_arr, gate_col_block, w_o, final_norm=None, name="out"):
    n_tok, d = x.shape
    width = o.shape[1]
    tm = min(512, n_tok)
    final = final_norm is not None
    in_specs = [
        pl.BlockSpec((tm, d), lambda i: (i, 0)),
        pl.BlockSpec((tm, width), lambda i: (i, 0)),
        pl.BlockSpec((tm, width), lambda i: (i, gate_col_block)),
        pl.BlockSpec((width, d), lambda i: (0, 0), pipeline_mode=pl.Buffered(1)),
    ]
    args = [x, o, gate_arr, w_o]
    if final:
        in_specs.append(pl.BlockSpec((1, d), lambda i: (0, 0)))
        args.append(final_norm.reshape(1, d).astype(F32))
    est = (width * d * 2 + 4 * tm * d * 4 + 4 * tm * width * 2 + 3 * tm * width * 4 + 2 * tm * d * 4)
    return pl.pallas_call(
        functools.partial(_out_kernel, final=final),
        grid=(n_tok // tm,),
        in_specs=in_specs,
        out_specs=pl.BlockSpec((tm, d), lambda i: (i, 0)),
        out_shape=jax.ShapeDtypeStruct((n_tok, d), F32),
        compiler_params=pltpu.CompilerParams(dimension_semantics=("parallel",),
                                             vmem_limit_bytes=_vmem_limit(est)),
        name=name,
    )(*args)


def _angles(pos, dim, theta):
    inv = theta ** (-jnp.arange(dim // 2, dtype=F32) * (2.0 / dim))
    ang = pos.astype(F32)[:, None] * inv[None, :]
    return jnp.cos(ang), jnp.sin(ang)


def _tables_a(seq_len, scale):
    c, s = _angles(jnp.arange(seq_len), A_ROPE, A_THETA)
    z32, z64 = jnp.zeros_like(c), jnp.zeros((seq_len, 64), F32)
    cos = jnp.concatenate([c, c, z64], axis=1)
    s1 = jnp.concatenate([z32, s, z64], axis=1)
    s2 = jnp.concatenate([-s, z32, z64], axis=1)
    return cos * scale, s1 * scale, s2 * scale


def _tables_b(seq_len, scale):
    c, s = _angles(jnp.arange(seq_len), B_ROT, B_THETA)
    z16, one, z96 = jnp.zeros_like(c), jnp.ones((seq_len, 96), F32), jnp.zeros((seq_len, 96), F32)
    cos = jnp.concatenate([c, c, one], axis=1)
    s1 = jnp.concatenate([z16, s, z96], axis=1)
    s2 = jnp.concatenate([-s, z16, z96], axis=1)
    return cos * scale, s1 * scale, s2 * scale


def _tables_c(seq_len):
    t = jnp.arange(seq_len)
    half = C_HEAD_DIM // 2
    cr, sr = _angles(t // GRID_W, half, C_THETA)
    cc, sc = _angles(t % GRID_W, half, C_THETA)
    z = jnp.zeros_like(cr)
    cos = jnp.concatenate([cr, cr, cc, cc], axis=1)
    s1 = jnp.concatenate([z, sr, z, sc], axis=1)
    s2 = jnp.concatenate([-sr, z, -sc, z], axis=1)
    return cos, s1, s2


def _layer_a(x, n_seq, seq_len, norm_w, w_in, q_norm, w_uq, kv_norm, w_ukv, w_o, final_norm, tag):
    d = x.shape[1]
    width = A_HEADS * A_V
    r0 = A_Q_RANK + A_KV_RANK
    w1 = jnp.concatenate([w_in[:, r0 + A_ROPE:], w_in[:, :r0], w_in[:, r0:r0 + A_ROPE],
                          jnp.zeros((d, LANES - A_ROPE), w_in.dtype)], axis=1).astype(BF16)
    n_plain = width + r0
    segs = [Seg("plain", 0, n_plain, 0, 0), Seg("rope", n_plain, LANES, 1, 0)]
    main, k_rope = _projection(x, 0, d, norm_w, w1, segs, [n_plain, LANES], seq_len,
                               tables=_tables_a(seq_len, 1.0), roll_by=A_ROPE // 2, name=f"a_in_{tag}")
    cq_block, ckv_block = width // A_Q_RANK, (width + A_Q_RANK) // A_KV_RANK

    scale = (A_NOPE + A_ROPE) ** -0.5 * LOG2E
    wq = w_uq.reshape(A_Q_RANK, A_HEADS, A_NOPE + A_ROPE)
    wq_rope = jnp.concatenate([wq[:, :, A_NOPE:], jnp.zeros((A_Q_RANK, A_HEADS, LANES - A_ROPE), wq.dtype)], axis=2)
    wq2 = jnp.concatenate([wq[:, :, :A_NOPE].reshape(A_Q_RANK, -1), wq_rope.reshape(A_Q_RANK, -1)],
                          axis=1).astype(BF16)
    hn = A_HEADS * LANES
    segs = [Seg("plain", 0, hn, 0, 0, scale=scale), Seg("rope", hn, hn, 1, 0)]
    q_nope, q_rope = _projection(main, cq_block, A_Q_RANK, q_norm, wq2, segs, [hn, hn], seq_len,
                                 tables=_tables_a(seq_len, scale), roll_by=A_ROPE // 2, name=f"a_q_{tag}")

    wkv = w_ukv.reshape(A_KV_RANK, A_HEADS, A_NOPE + A_V)
    wkv2 = jnp.concatenate([wkv[:, :, :A_NOPE].reshape(A_KV_RANK, -1), wkv[:, :, A_NOPE:].reshape(A_KV_RANK, -1)],
                           axis=1).astype(BF16)
    segs = [Seg("plain", 0, hn, 0, 0), Seg("plain", hn, hn, 1, 0)]
    k_nope, v = _projection(main, ckv_block, A_KV_RANK, kv_norm, wkv2, segs, [hn, hn], seq_len, name=f"a_kv_{tag}")

    o = _dense_attention([q_nope, q_rope], [k_nope, k_rope], v, n_seq, seq_len, A_HEADS, 1, name=f"a_attn_{tag}")
    return _output(x, o, main, 0, w_o.astype(BF16), final_norm, name=f"a_out_{tag}")


def _gqa_projection(x, seq_len, norm_w, w_in, heads, kv_heads, kind, tables, gains, q_scale, roll_by, name):
    d = x.shape[1]
    qw, kw = heads * LANES, kv_heads * LANES
    segs = [Seg(kind, 0, qw, 0, 0, scale=q_scale, gain_idx=0), Seg(kind, qw, kw, 1, 0, gain_idx=1),
            Seg("plain", qw + kw, kw, 2, 0), Seg("plain", qw + 2 * kw, qw, 3, 0)]
    return _projection(x, 0, d, norm_w, w_in.astype(BF16), segs, [qw, kw, kw, qw], seq_len,
                       tables=tables, gains=gains, roll_by=roll_by, name=name)


def _layer_b(x, n_seq, seq_len, norm_w, w_in, sink, w_o, final_norm, tag):
    scale = B_HEAD_DIM ** -0.5 * LOG2E
    tq = _tables_b(seq_len, 1.0)
    q, k, v, gate = _gqa_projection(x, seq_len, norm_w, w_in, B_HEADS, B_KV_HEADS, "rope", tq, (), scale,
                                    B_ROT // 2, f"b_in_{tag}")
    o = _banded_attention(q, k, v, sink, n_seq, seq_len, name=f"b_attn_{tag}")
    return _output(x, o, gate, 0, w_o.astype(BF16), final_norm, name=f"b_out_{tag}")


def _layer_c(x, n_seq, seq_len, norm_w, w_in, q_gain, k_gain, w_o, final_norm, tag):
    scale = C_HEAD_DIM ** -0.5 * LOG2E
    gains = (q_gain.reshape(1, LANES).astype(F32), k_gain.reshape(1, LANES).astype(F32))
    q, k, v, gate = _gqa_projection(x, seq_len, norm_w, w_in, C_HEADS, C_KV_HEADS, "hnorm_rope",
                                    _tables_c(seq_len), gains, scale, C_HEAD_DIM // 4, f"c_in_{tag}")
    o = _dense_attention([q], [k], v, n_seq, seq_len, C_KV_HEADS, C_HEADS // C_KV_HEADS, name=f"c_attn_{tag}")
    return _output(x, o, gate, 0, w_o.astype(BF16), final_norm, name=f"c_out_{tag}")


def _trunk(x3, tag, norm_w, a_w_in, a_q_norm, a_w_uq, a_kv_norm, a_w_ukv, a_w_o, b_w_in, b_sink, b_w_o,
           c_w_in, c_q_scale, c_k_scale, c_w_o, final_norm):
    n_seq, seq_len, d = x3.shape
    depth = norm_w.shape[0]
    x = x3.reshape(n_seq * seq_len, d)
    for i in range(depth):
        kind, j = i % 3, i // 3
        fn = final_norm if i == depth - 1 else None
        if kind == 0:
            x = _layer_a(x, n_seq, seq_len, norm_w[i], a_w_in[j], a_q_norm[j], a_w_uq[j], a_kv_norm[j],
                         a_w_ukv[j], a_w_o[j], fn, f"{tag}{i}")
        elif kind == 1:
            x = _layer_b(x, n_seq, seq_len, norm_w[i], b_w_in[j], b_sink[j], b_w_o[j], fn, f"{tag}{i}")
        else:
            x = _layer_c(x, n_seq, seq_len, norm_w[i], c_w_in[j], c_q_scale[j], c_k_scale[j], c_w_o[j], fn,
                         f"{tag}{i}")
    return x.reshape(n_seq, seq_len, d)


def kernel(x_prompt, x_sample, norm_w, a_w_in, a_q_norm, a_w_uq, a_kv_norm, a_w_ukv, a_w_o, b_w_in, b_sink,
           b_w_o, c_w_in, c_q_scale, c_k_scale, c_w_o, final_norm):
    params = (norm_w, a_w_in, a_q_norm, a_w_uq, a_kv_norm, a_w_ukv, a_w_o, b_w_in, b_sink, b_w_o,
              c_w_in, c_q_scale, c_k_scale, c_w_o, final_norm)
    return (_trunk(x_prompt, "p", *params), _trunk(x_sample, "s", *params))
```

```python
import dataclasses
import functools
import math

import jax
import jax.numpy as jnp
from jax import lax
from jax.experimental import pallas as pl
from jax.experimental.pallas import tpu as pltpu

F32 = jnp.float32
BF16 = jnp.bfloat16

LANES = 128
V7X_VMEM_BYTES = 64 * 1024 * 1024
VMEM_CEILING = V7X_VMEM_BYTES - 8 * 1024 * 1024

EPS = 1e-6
NEG = -1e30
LOG2E = math.log2(math.e)
A_HEADS, A_Q_RANK, A_KV_RANK, A_NOPE, A_ROPE, A_V = 16, 512, 512, 128, 64, 128
A_THETA = 10000.0
B_HEADS, B_KV_HEADS, B_HEAD_DIM, B_WINDOW, B_ROT = 16, 4, 128, 128, 32
B_THETA = 500000.0
C_HEADS, C_KV_HEADS, C_HEAD_DIM, GRID_W = 16, 4, 128, 64
C_THETA = 10000.0

DOT_COLS = 512


def _vmem_limit(estimate_bytes):
    return int(min(max(estimate_bytes, 16 * 1024 * 1024), VMEM_CEILING))


@dataclasses.dataclass(frozen=True)
class Seg:
    kind: str
    w_col: int
    width: int
    out_idx: int
    out_col: int
    scale: float = 1.0
    gain_idx: int = -1


def _proj_kernel(*refs, segs, n_tables, n_gains, n_out, roll_by):
    x_ref, gamma_ref, w_ref = refs[:3]
    pos = 3
    tables = refs[pos:pos + n_tables]
    pos += n_tables
    gains = refs[pos:pos + n_gains]
    pos += n_gains
    outs = refs[pos:pos + n_out]
    xn_ref = refs[pos + n_out]

    x = x_ref[...].astype(F32)
    ms = jnp.mean(x * x, axis=-1, keepdims=True)
    xn_ref[...] = (x * lax.rsqrt(ms + EPS) * gamma_ref[...]).astype(BF16)

    for seg in segs:
        out_ref = outs[seg.out_idx]
        for c0 in range(0, seg.width, DOT_COLS):
            w = min(DOT_COLS, seg.width - c0)
            acc = jnp.dot(xn_ref[...], w_ref[:, seg.w_col + c0:seg.w_col + c0 + w],
                          preferred_element_type=F32)
            if seg.kind == "plain":
                if seg.scale != 1.0:
                    acc = acc * seg.scale
                out_ref[:, seg.out_col + c0:seg.out_col + c0 + w] = acc.astype(out_ref.dtype)
                continue
            cos_ref, s1_ref, s2_ref = tables
            for l0 in range(0, w, LANES):
                a = acc[:, l0:l0 + LANES]
                if seg.kind == "hnorm_rope":
                    hms = jnp.mean(a * a, axis=-1, keepdims=True)
                    a = a * lax.rsqrt(hms + EPS) * gains[seg.gain_idx][...]
                a = (a * cos_ref[...] + pltpu.roll(a, roll_by, 1) * s1_ref[...]
                     + pltpu.roll(a, LANES - roll_by, 1) * s2_ref[...])
                if seg.scale != 1.0:
                    a = a * seg.scale
                o0 = seg.out_col + c0 + l0
                out_ref[:, o0:o0 + LANES] = a.astype(out_ref.dtype)


def _proj_tile(n_tok, k_dim, w_cols, out_cols, x_bytes):
    for tm in (512, 256, 128):
        if n_tok % tm:
            continue
        est = (k_dim * w_cols * 2 + 2 * tm * k_dim * x_bytes + 2 * tm * out_cols * 2
               + tm * k_dim * 2 + 3 * tm * DOT_COLS * 4 + 6 * tm * LANES * 4)
        if est <= VMEM_CEILING - 4 * 1024 * 1024:
            return tm, est
    raise ValueError("projection does not fit VMEM")


def _projection(x, x_col_block, k_dim, gamma, w, segs, out_widths, seq_len, tables=(), gains=(),
                roll_by=0, name="proj"):
    n_tok = x.shape[0]
    w_cols = w.shape[1]
    tm, est = _proj_tile(n_tok, k_dim, w_cols, sum(out_widths), x.dtype.itemsize)
    tm = min(tm, seq_len)
    pos_blocks = seq_len // tm
    in_specs = [
        pl.BlockSpec((tm, k_dim), lambda i: (i, x_col_block)),
        pl.BlockSpec((1, k_dim), lambda i: (0, 0)),
        pl.BlockSpec((k_dim, w_cols), lambda i: (0, 0), pipeline_mode=pl.Buffered(1)),
    ]
    in_specs += [pl.BlockSpec((tm, LANES), lambda i: (i % pos_blocks, 0)) for _ in tables]
    in_specs += [pl.BlockSpec((1, LANES), lambda i: (0, 0)) for _ in gains]
    out_specs = [pl.BlockSpec((tm, ow), lambda i: (i, 0)) for ow in out_widths]
    out_shape = [jax.ShapeDtypeStruct((n_tok, ow), BF16) for ow in out_widths]
    kern = functools.partial(_proj_kernel, segs=tuple(segs), n_tables=len(tables), n_gains=len(gains),
                             n_out=len(out_widths), roll_by=roll_by)
    return pl.pallas_call(
        kern,
        grid=(n_tok // tm,),
        in_specs=in_specs,
        out_specs=out_specs,
        out_shape=out_shape,
        scratch_shapes=[pltpu.VMEM((tm, k_dim), BF16)],
        compiler_params=pltpu.CompilerParams(dimension_semantics=("parallel",),
                                             vmem_limit_bytes=_vmem_limit(est)),
        name=name,
    )(x, gamma.reshape(1, k_dim).astype(F32), w, *tables, *gains)


def _dense_attn_kernel(*refs, n_q, n_k, group, tk, seq_len, col_blocks):
    q_refs = refs[:n_q]
    k_refs = refs[n_q:n_q + n_k]
    v_ref = refs[n_q + n_k]
    o_ref = refs[n_q + n_k + 1]
    vt_ref, st_a, st_b, acc_ref = refs[n_q + n_k + 2:]
    tq = o_ref.shape[0]
    rows = group * tq
    n_chunks = seq_len // tk
    col_w = rows // col_blocks

    @pl.when(pl.program_id(2) == 0)
    def _():
        def tbody(c, carry):
            start = pl.multiple_of(c * tk, tk)
            vt_ref[:, pl.ds(start, tk)] = v_ref[pl.ds(start, tk), :].T
            return carry
        lax.fori_loop(0, n_chunks, tbody, 0)

    def stack_heads(ref):
        return jnp.concatenate([ref[:, g * LANES:(g + 1) * LANES] for g in range(group)], axis=0)

    q = jnp.concatenate([stack_heads(r) for r in q_refs], axis=1)

    def scores_t(j, dst):
        start = pl.multiple_of(j * tk, tk)
        k = jnp.concatenate([r[pl.ds(start, tk), :] for r in k_refs], axis=1)
        dst[...] = lax.dot_general(k, q, (((1,), (1,)), ((), ())), preferred_element_type=F32)

    def softmax_pv(j, src, m, l):
        start = pl.multiple_of(j * tk, tk)
        vt = vt_ref[:, pl.ds(start, tk)]
        m_out, l_out = [], []
        for c in range(col_blocks):
            cs = slice(c * col_w, (c + 1) * col_w)
            st = src[:, cs]
            m_new = jnp.maximum(m[c], jnp.max(st, axis=0, keepdims=True))
            alpha = jnp.exp2(m[c] - m_new)
            pt = jnp.exp2(st - m_new)
            l_out.append(alpha * l[c] + jnp.sum(pt, axis=0, keepdims=True))
            m_out.append(m_new)
            acc_ref[:, cs] = alpha * acc_ref[:, cs] + jnp.dot(vt, pt.astype(BF16),
                                                              preferred_element_type=F32)
        return tuple(m_out), tuple(l_out)

    acc_ref[...] = jnp.zeros_like(acc_ref)
    scores_t(0, st_a)

    def pair(jj, carry):
        m, l = carry
        j0 = 2 * jj
        scores_t(j0 + 1, st_b)
        m, l = softmax_pv(j0, st_a, m, l)
        scores_t(jnp.minimum(j0 + 2, n_chunks - 1), st_a)
        m, l = softmax_pv(j0 + 1, st_b, m, l)
        return m, l

    init = (tuple(jnp.full((1, col_w), -jnp.inf, F32) for _ in range(col_blocks)),
            tuple(jnp.zeros((1, col_w), F32) for _ in range(col_blocks)))
    _, l = lax.fori_loop(0, n_chunks // 2, pair, init)
    out = (acc_ref[...] / jnp.concatenate(l, axis=1)).T
    for g in range(group):
        o_ref[:, g * LANES:(g + 1) * LANES] = out[g * tq:(g + 1) * tq, :].astype(o_ref.dtype)


def _dense_attention(q_list, k_list, v, n_seq, seq_len, n_kv_heads, group, name):
    n_tok = n_seq * seq_len
    tq = min(512 // group, seq_len)
    tk = min(1024, seq_len // 2)
    assert seq_len % (2 * tk) == 0 and seq_len % tq == 0
    q_blocks = seq_len // tq
    gw = group * LANES

    q_spec = pl.BlockSpec((tq, gw), lambda b, h, i: (b * q_blocks + i, h))
    in_specs = [q_spec for _ in q_list]
    for k in k_list:
        if k.shape[1] == LANES:
            in_specs.append(pl.BlockSpec((seq_len, LANES), lambda b, h, i: (b, 0)))
        else:
            in_specs.append(pl.BlockSpec((seq_len, LANES), lambda b, h, i: (b, h)))
    in_specs.append(pl.BlockSpec((seq_len, LANES), lambda b, h, i: (b, h)))
    rows = group * tq
    est = (2 * len(q_list) * tq * gw * 2 + 2 * (len(k_list) + 1) * seq_len * LANES * 2 + 2 * tq * gw * 2
           + seq_len * LANES * 2 + 5 * rows * tk * 4 + 4 * rows * LANES * 4)
    kern = functools.partial(_dense_attn_kernel, n_q=len(q_list), n_k=len(k_list), group=group, tk=tk,
                             seq_len=seq_len, col_blocks=2)
    return pl.pallas_call(
        kern,
        grid=(n_seq, n_kv_heads, q_blocks),
        in_specs=in_specs,
        out_specs=q_spec,
        out_shape=jax.ShapeDtypeStruct((n_tok, n_kv_heads * gw), BF16),
        scratch_shapes=[pltpu.VMEM((LANES, seq_len), BF16), pltpu.VMEM((tk, rows), F32),
                        pltpu.VMEM((tk, rows), F32), pltpu.VMEM((LANES, rows), F32)],
        compiler_params=pltpu.CompilerParams(dimension_semantics=("parallel", "parallel", "arbitrary"),
                                             vmem_limit_bytes=_vmem_limit(est)),
        name=name,
    )(*q_list, *k_list, v)


def _banded_attn_kernel(sink_ref, q_ref, k_ref, v_ref, o_ref, vt_ref, *, group, qb, window, seq_len, unroll):
    tq = o_ref.shape[0]
    n_sub = tq // qb
    span = qb + 2 * window
    h = pl.program_id(1)
    i = pl.program_id(2)
    vt_chunk = min(512, seq_len)

    @pl.when(i == 0)
    def _():
        def tbody(c, carry):
            start = pl.multiple_of(c * vt_chunk, vt_chunk)
            vt_ref[:, pl.ds(start, vt_chunk)] = v_ref[pl.ds(start, vt_chunk), :].T
            return carry
        lax.fori_loop(0, seq_len // vt_chunk, tbody, 0)

    sink_row = jnp.concatenate(
        [jnp.full((1, qb), sink_ref[h * group + g] * LOG2E, F32) for g in range(group)], axis=1)
    rel0 = (lax.broadcasted_iota(jnp.int32, (span, qb), 0) - lax.broadcasted_iota(jnp.int32, (span, qb), 1))

    def one_block(sb):
        q0 = pl.multiple_of(sb * qb, qb)
        q_abs = i * tq + q0
        k0 = pl.multiple_of(jnp.clip(q_abs - window, 0, seq_len - span), LANES)
        rel = rel0 + (k0 - q_abs)
        bias = jnp.where(jnp.abs(rel) <= window, 0.0, NEG).astype(F32)
        q = jnp.concatenate([q_ref[pl.ds(q0, qb), g * LANES:(g + 1) * LANES] for g in range(group)], axis=0)
        k = k_ref[pl.ds(k0, span), :]
        st = lax.dot_general(k, q, (((1,), (1,)), ((), ())), preferred_element_type=F32)
        st = st + jnp.concatenate([bias] * group, axis=1)
        m = jnp.maximum(jnp.max(st, axis=0, keepdims=True), sink_row)
        pt = jnp.exp2(st - m)
        denom = jnp.sum(pt, axis=0, keepdims=True) + jnp.exp2(sink_row - m)
        out_t = jnp.dot(vt_ref[:, pl.ds(k0, span)], pt.astype(BF16), preferred_element_type=F32) / denom
        out = out_t.T
        for g in range(group):
            o_ref[pl.ds(q0, qb), g * LANES:(g + 1) * LANES] = out[g * qb:(g + 1) * qb, :].astype(o_ref.dtype)

    def body(sbb, carry):
        for u in range(unroll):
            one_block(sbb * unroll + u)
        return carry

    lax.fori_loop(0, n_sub // unroll, body, 0)


def _banded_attention(q, k, v, sink, n_seq, seq_len, name):
    n_tok = n_seq * seq_len
    group = B_HEADS // B_KV_HEADS
    gw = group * LANES
    qb, window = 128, B_WINDOW
    tq = min(1024, seq_len)
    q_blocks = seq_len // tq
    q_spec = pl.BlockSpec((tq, gw), lambda b, h, i: (b * q_blocks + i, h))
    kv_spec = pl.BlockSpec((seq_len, LANES), lambda b, h, i: (b, h))
    span = qb + 2 * window
    unroll = 2
    assert seq_len >= span and (tq // qb) % unroll == 0
    est = 4 * tq * gw * 2 + 5 * seq_len * LANES * 2 + 6 * unroll * group * qb * span * 4
    kern = functools.partial(_banded_attn_kernel, group=group, qb=qb, window=window, seq_len=seq_len,
                             unroll=unroll)
    return pl.pallas_call(
        kern,
        grid=(n_seq, B_KV_HEADS, q_blocks),
        in_specs=[pl.BlockSpec(memory_space=pltpu.SMEM), q_spec, kv_spec, kv_spec],
        out_specs=q_spec,
        out_shape=jax.ShapeDtypeStruct((n_tok, B_HEADS * LANES), BF16),
        scratch_shapes=[pltpu.VMEM((LANES, seq_len), BF16)],
        compiler_params=pltpu.CompilerParams(dimension_semantics=("parallel", "parallel", "arbitrary"),
                                             vmem_limit_bytes=_vmem_limit(est)),
        name=name,
    )(sink.astype(F32), q, k, v)


def _out_kernel(*refs, final):
    if final:
        x_ref, o_ref, g_ref, w_ref, fn_ref, y_ref = refs
    else:
        x_ref, o_ref, g_ref, w_ref, y_ref = refs
    g = g_ref[...].astype(F32)
    u = (o_ref[...].astype(F32) * (g / (1.0 + jnp.exp(-g)))).astype(BF16)
    y = x_ref[...] + jnp.dot(u, w_ref[...], preferred_element_type=F32)
    if final:
        ms = jnp.mean(y * y, axis=-1, keepdims=True)
        y = y * lax.rsqrt(ms + EPS) * fn_ref[...]
    y_ref[...] = y


def _output(x, o, gate_arr, gate_col_block, w_o, final_norm=None, name="out"):
    n_tok, d = x.shape
    width = o.shape[1]
    tm = min(512, n_tok)
    final = final_norm is not None
    in_specs = [
        pl.BlockSpec((tm, d), lambda i: (i, 0)),
        pl.BlockSpec((tm, width), lambda i: (i, 0)),
        pl.BlockSpec((tm, width), lambda i: (i, gate_col_block)),
        pl.BlockSpec((width, d), lambda i: (0, 0), pipeline_mode=pl.Buffered(1)),
    ]
    args = [x, o, gate_arr, w_o]
    if final:
        in_specs.append(pl.BlockSpec((1, d), lambda i: (0, 0)))
        args.append(final_norm.reshape(1, d).astype(F32))
    est = (width * d * 2 + 4 * tm * d * 4 + 4 * tm * width * 2 + 3 * tm * width * 4 + 2 * tm * d * 4)
    return pl.pallas_call(
        functools.partial(_out_kernel, final=final),
        grid=(n_tok // tm,),
        in_specs=in_specs,
        out_specs=pl.BlockSpec((tm, d), lambda i: (i, 0)),
        out_shape=jax.ShapeDtypeStruct((n_tok, d), F32),
        compiler_params=pltpu.CompilerParams(dimension_semantics=("parallel",),
                                             vmem_limit_bytes=_vmem_limit(est)),
        name=name,
    )(*args)


def _angles(pos, dim, theta):
    inv = theta ** (-jnp.arange(dim // 2, dtype=F32) * (2.0 / dim))
    ang = pos.astype(F32)[:, None] * inv[None, :]
    return jnp.cos(ang), jnp.sin(ang)


def _tables_a(seq_len, scale):
    c, s = _angles(jnp.arange(seq_len), A_ROPE, A_THETA)
    z32, z64 = jnp.zeros_like(c), jnp.zeros((seq_len, 64), F32)
    cos = jnp.concatenate([c, c, z64], axis=1)
    s1 = jnp.concatenate([z32, s, z64], axis=1)
    s2 = jnp.concatenate([-s, z32, z64], axis=1)
    return cos * scale, s1 * scale, s2 * scale


def _tables_b(seq_len, scale):
    c, s = _angles(jnp.arange(seq_len), B_ROT, B_THETA)
    z16, one, z96 = jnp.zeros_like(c), jnp.ones((seq_len, 96), F32), jnp.zeros((seq_len, 96), F32)
    cos = jnp.concatenate([c, c, one], axis=1)
    s1 = jnp.concatenate([z16, s, z96], axis=1)
    s2 = jnp.concatenate([-s, z16, z96], axis=1)
    return cos * scale, s1 * scale, s2 * scale


def _tables_c(seq_len):
    t = jnp.arange(seq_len)
    half = C_HEAD_DIM // 2
    cr, sr = _angles(t // GRID_W, half, C_THETA)
    cc, sc = _angles(t % GRID_W, half, C_THETA)
    z = jnp.zeros_like(cr)
    cos = jnp.concatenate([cr, cr, cc, cc], axis=1)
    s1 = jnp.concatenate([z, sr, z, sc], axis=1)
    s2 = jnp.concatenate([-sr, z, -sc, z], axis=1)
    return cos, s1, s2


def _layer_a(x, n_seq, seq_len, norm_w, w_in, q_norm, w_uq, kv_norm, w_ukv, w_o, final_norm, tag):
    d = x.shape[1]
    width = A_HEADS * A_V
    r0 = A_Q_RANK + A_KV_RANK
    w1 = jnp.concatenate([w_in[:, r0 + A_ROPE:], w_in[:, :r0], w_in[:, r0:r0 + A_ROPE],
                          jnp.zeros((d, LANES - A_ROPE), w_in.dtype)], axis=1).astype(BF16)
    n_plain = width + r0
    segs = [Seg("plain", 0, n_plain, 0, 0), Seg("rope", n_plain, LANES, 1, 0)]
    main, k_rope = _projection(x, 0, d, norm_w, w1, segs, [n_plain, LANES], seq_len,
                               tables=_tables_a(seq_len, 1.0), roll_by=A_ROPE // 2, name=f"a_in_{tag}")
    cq_block, ckv_block = width // A_Q_RANK, (width + A_Q_RANK) // A_KV_RANK

    scale = (A_NOPE + A_ROPE) ** -0.5 * LOG2E
    wq = w_uq.reshape(A_Q_RANK, A_HEADS, A_NOPE + A_ROPE)
    wq_rope = jnp.concatenate([wq[:, :, A_NOPE:], jnp.zeros((A_Q_RANK, A_HEADS, LANES - A_ROPE), wq.dtype)], axis=2)
    wq2 = jnp.concatenate([wq[:, :, :A_NOPE].reshape(A_Q_RANK, -1), wq_rope.reshape(A_Q_RANK, -1)],
                          axis=1).astype(BF16)
    hn = A_HEADS * LANES
    segs = [Seg("plain", 0, hn, 0, 0, scale=scale), Seg("rope", hn, hn, 1, 0)]
    q_nope, q_rope = _projection(main, cq_block, A_Q_RANK, q_norm, wq2, segs, [hn, hn], seq_len,
                                 tables=_tables_a(seq_len, scale), roll_by=A_ROPE // 2, name=f"a_q_{tag}")

    wkv = w_ukv.reshape(A_KV_RANK, A_HEADS, A_NOPE + A_V)
    wkv2 = jnp.concatenate([wkv[:, :, :A_NOPE].reshape(A_KV_RANK, -1), wkv[:, :, A_NOPE:].reshape(A_KV_RANK, -1)],
                           axis=1).astype(BF16)
    segs = [Seg("plain", 0, hn, 0, 0), Seg("plain", hn, hn, 1, 0)]
    k_nope, v = _projection(main, ckv_block, A_KV_RANK, kv_norm, wkv2, segs, [hn, hn], seq_len, name=f"a_kv_{tag}")

    o = _dense_attention([q_nope, q_rope], [k_nope, k_rope], v, n_seq, seq_len, A_HEADS, 1, name=f"a_attn_{tag}")
    return _output(x, o, main, 0, w_o.astype(BF16), final_norm, name=f"a_out_{tag}")


def _gqa_projection(x, seq_len, norm_w, w_in, heads, kv_heads, kind, tables, gains, q_scale, roll_by, name):
    d = x.shape[1]
    qw, kw = heads * LANES, kv_heads * LANES
    segs = [Seg(kind, 0, qw, 0, 0, scale=q_scale, gain_idx=0), Seg(kind, qw, kw, 1, 0, gain_idx=1),
            Seg("plain", qw + kw, kw, 2, 0), Seg("plain", qw + 2 * kw, qw, 3, 0)]
    return _projection(x, 0, d, norm_w, w_in.astype(BF16), segs, [qw, kw, kw, qw], seq_len,
                       tables=tables, gains=gains, roll_by=roll_by, name=name)


def _layer_b(x, n_seq, seq_len, norm_w, w_in, sink, w_o, final_norm, tag):
    scale = B_HEAD_DIM ** -0.5 * LOG2E
    tq = _tables_b(seq_len, 1.0)
    q, k, v, gate = _gqa_projection(x, seq_len, norm_w, w_in, B_HEADS, B_KV_HEADS, "rope", tq, (), scale,
                                    B_ROT // 2, f"b_in_{tag}")
    o = _banded_attention(q, k, v, sink, n_seq, seq_len, name=f"b_attn_{tag}")
    return _output(x, o, gate, 0, w_o.astype(BF16), final_norm, name=f"b_out_{tag}")


def _layer_c(x, n_seq, seq_len, norm_w, w_in, q_gain, k_gain, w_o, final_norm, tag):
    scale = C_HEAD_DIM ** -0.5 * LOG2E
    gains = (q_gain.reshape(1, LANES).astype(F32), k_gain.reshape(1, LANES).astype(F32))
    q, k, v, gate = _gqa_projection(x, seq_len, norm_w, w_in, C_HEADS, C_KV_HEADS, "hnorm_rope",
                                    _tables_c(seq_len), gains, scale, C_HEAD_DIM // 4, f"c_in_{tag}")
    o = _dense_attention([q], [k], v, n_seq, seq_len, C_KV_HEADS, C_HEADS // C_KV_HEADS, name=f"c_attn_{tag}")
    return _output(x, o, gate, 0, w_o.astype(BF16), final_norm, name=f"c_out_{tag}")


def _trunk(x3, tag, norm_w, a_w_in, a_q_norm, a_w_uq, a_kv_norm, a_w_ukv, a_w_o, b_w_in, b_sink, b_w_o,
           c_w_in, c_q_scale, c_k_scale, c_w_o, final_norm):
    n_seq, seq_len, d = x3.shape
    depth = norm_w.shape[0]
    x = x3.reshape(n_seq * seq_len, d)
    for i in range(depth):
        kind, j = i % 3, i // 3
        fn = final_norm if i == depth - 1 else None
        if kind == 0:
            x = _layer_a(x, n_seq, seq_len, norm_w[i], a_w_in[j], a_q_norm[j], a_w_uq[j], a_kv_norm[j],
                         a_w_ukv[j], a_w_o[j], fn, f"{tag}{i}")
        elif kind == 1:
            x = _layer_b(x, n_seq, seq_len, norm_w[i], b_w_in[j], b_sink[j], b_w_o[j], fn, f"{tag}{i}")
        else:
            x = _layer_c(x, n_seq, seq_len, norm_w[i], c_w_in[j], c_q_scale[j], c_k_scale[j], c_w_o[j], fn,
                         f"{tag}{i}")
    return x.reshape(n_seq, seq_len, d)


def kernel(x_prompt, x_sample, norm_w, a_w_in, a_q_norm, a_w_uq, a_kv_norm, a_w_ukv, a_w_o, b_w_in, b_sink,
           b_w_o, c_w_in, c_q_scale, c_k_scale, c_w_o, final_norm):
    params = (norm_w, a_w_in, a_q_norm, a_w_uq, a_kv_norm, a_w_ukv, a_w_o, b_w_in, b_sink, b_w_o,
              c_w_in, c_q_scale, c_k_scale, c_w_o, final_norm)
    return (_trunk(x_prompt, "p", *params), _trunk(x_sample, "s", *params))
```

```python
import dataclasses
import functools
import math

import jax
import jax.numpy as jnp
from jax import lax
from jax.experimental import pallas as pl
from jax.experimental.pallas import tpu as pltpu

F32 = jnp.float32
BF16 = jnp.bfloat16

LANES = 128
V7X_VMEM_BYTES = 64 * 1024 * 1024
VMEM_CEILING = V7X_VMEM_BYTES - 8 * 1024 * 1024

EPS = 1e-6
NEG = -1e30
LOG2E = math.log2(math.e)
A_HEADS, A_Q_RANK, A_KV_RANK, A_NOPE, A_ROPE, A_V = 16, 512, 512, 128, 64, 128
A_THETA = 10000.0
B_HEADS, B_KV_HEADS, B_HEAD_DIM, B_WINDOW, B_ROT = 16, 4, 128, 128, 32
B_THETA = 500000.0
C_HEADS, C_KV_HEADS, C_HEAD_DIM, GRID_W = 16, 4, 128, 64
C_THETA = 10000.0

DOT_COLS = 512


def _vmem_limit(estimate_bytes):
    return int(min(max(estimate_bytes, 16 * 1024 * 1024), VMEM_CEILING))


@dataclasses.dataclass(frozen=True)
class Seg:
    kind: str
    w_col: int
    width: int
    out_idx: int
    out_col: int
    scale: float = 1.0
    gain_idx: int = -1


def _proj_kernel(*refs, segs, n_tables, n_gains, n_out, roll_by):
    x_ref, gamma_ref, w_ref = refs[:3]
    pos = 3
    tables = refs[pos:pos + n_tables]
    pos += n_tables
    gains = refs[pos:pos + n_gains]
    pos += n_gains
    outs = refs[pos:pos + n_out]
    xn_ref = refs[pos + n_out]

    x = x_ref[...].astype(F32)
    ms = jnp.mean(x * x, axis=-1, keepdims=True)
    xn_ref[...] = (x * lax.rsqrt(ms + EPS) * gamma_ref[...]).astype(BF16)

    for seg in segs:
        out_ref = outs[seg.out_idx]
        for c0 in range(0, seg.width, DOT_COLS):
            w = min(DOT_COLS, seg.width - c0)
            acc = jnp.dot(xn_ref[...], w_ref[:, seg.w_col + c0:seg.w_col + c0 + w],
                          preferred_element_type=F32)
            if seg.kind == "plain":
                if seg.scale != 1.0:
                    acc = acc * seg.scale
                out_ref[:, seg.out_col + c0:seg.out_col + c0 + w] = acc.astype(out_ref.dtype)
                continue
            cos_ref, s1_ref, s2_ref = tables
            for l0 in range(0, w, LANES):
                a = acc[:, l0:l0 + LANES]
                if seg.kind == "hnorm_rope":
                    hms = jnp.mean(a * a, axis=-1, keepdims=True)
                    a = a * lax.rsqrt(hms + EPS) * gains[seg.gain_idx][...]
                a = (a * cos_ref[...] + pltpu.roll(a, roll_by, 1) * s1_ref[...]
                     + pltpu.roll(a, LANES - roll_by, 1) * s2_ref[...])
                if seg.scale != 1.0:
                    a = a * seg.scale
                o0 = seg.out_col + c0 + l0
                out_ref[:, o0:o0 + LANES] = a.astype(out_ref.dtype)


def _proj_tile(n_tok, k_dim, w_cols, out_cols, x_bytes):
    for tm in (512, 256, 128):
        if n_tok % tm:
            continue
        est = (k_dim * w_cols * 2 + 2 * tm * k_dim * x_bytes + 2 * tm * out_cols * 2
               + tm * k_dim * 2 + 3 * tm * DOT_COLS * 4 + 6 * tm * LANES * 4)
        if est <= VMEM_CEILING - 4 * 1024 * 1024:
            return tm, est
    raise ValueError("projection does not fit VMEM")


def _projection(x, x_col_block, k_dim, gamma, w, segs, out_widths, seq_len, tables=(), gains=(),
                roll_by=0, name="proj"):
    n_tok = x.shape[0]
    w_cols = w.shape[1]
    tm, est = _proj_tile(n_tok, k_dim, w_cols, sum(out_widths), x.dtype.itemsize)
    tm = min(tm, seq_len)
    pos_blocks = seq_len // tm
    in_specs = [
        pl.BlockSpec((tm, k_dim), lambda i: (i, x_col_block)),
        pl.BlockSpec((1, k_dim), lambda i: (0, 0)),
        pl.BlockSpec((k_dim, w_cols), lambda i: (0, 0), pipeline_mode=pl.Buffered(1)),
    ]
    in_specs += [pl.BlockSpec((tm, LANES), lambda i: (i % pos_blocks, 0)) for _ in tables]
    in_specs += [pl.BlockSpec((1, LANES), lambda i: (0, 0)) for _ in gains]
    out_specs = [pl.BlockSpec((tm, ow), lambda i: (i, 0)) for ow in out_widths]
    out_shape = [jax.ShapeDtypeStruct((n_tok, ow), BF16) for ow in out_widths]
    kern = functools.partial(_proj_kernel, segs=tuple(segs), n_tables=len(tables), n_gains=len(gains),
                             n_out=len(out_widths), roll_by=roll_by)
    return pl.pallas_call(
        kern,
        grid=(n_tok // tm,),
        in_specs=in_specs,
        out_specs=out_specs,
        out_shape=out_shape,
        scratch_shapes=[pltpu.VMEM((tm, k_dim), BF16)],
        compiler_params=pltpu.CompilerParams(dimension_semantics=("parallel",),
                                             vmem_limit_bytes=_vmem_limit(est)),
        name=name,
    )(x, gamma.reshape(1, k_dim).astype(F32), w, *tables, *gains)


def _dense_attn_kernel(*refs, n_q, n_k, group, tk, seq_len):
    q_refs = refs[:n_q]
    qn_refs = refs[n_q:2 * n_q]
    k_refs = refs[2 * n_q:2 * n_q + n_k]
    v_ref = refs[2 * n_q + n_k]
    o_ref = refs[2 * n_q + n_k + 1]
    vt_ref, st0, st1, pt0, pt1, cm_ref = refs[2 * n_q + n_k + 2:]
    st_bufs, pt_bufs = (st0, st1), (pt0, pt1)
    tq = o_ref.shape[0]
    rows = group * tq
    n_chunks = seq_len // tk
    assert n_chunks % 2 == 0

    @pl.when(pl.program_id(2) == 0)
    def _():
        def tbody(c, carry):
            start = pl.multiple_of(c * tk, tk)
            vt_ref[:, pl.ds(start, tk)] = v_ref[pl.ds(start, tk), :].T
            return carry
        lax.fori_loop(0, n_chunks, tbody, 0)

    def stack_heads(ref):
        return jnp.concatenate([ref[:, g * LANES:(g + 1) * LANES] for g in range(group)], axis=0)

    def queries(q_list):
        return jnp.concatenate([stack_heads(r) for r in q_list], axis=1)

    def scores_t(j, q, dst):
        k = jnp.concatenate([r[j * tk:(j + 1) * tk, :] for r in k_refs], axis=1)
        st = lax.dot_general(k, q, (((1,), (1,)), ((), ())), preferred_element_type=F32)
        dst[...] = st
        return jnp.max(st, axis=0, keepdims=True)

    @pl.when(pl.program_id(2) == 0)
    def _():
        cm_ref[...] = scores_t(0, queries(q_refs), st0)

    m = jnp.full((1, rows), -jnp.inf, F32)
    l8 = jnp.zeros((8, rows), F32)
    acc = jnp.zeros((LANES, rows), F32)
    cmax = cm_ref[...]
    alpha_prev = None
    for s in range(n_chunks + 1):
        alpha = None
        if s < n_chunks:
            m_new = jnp.maximum(m, cmax)
            alpha = jnp.exp2(m - m_new)
            p = jnp.exp2(st_bufs[s % 2][...] - m_new)
            l8 = alpha * l8 + jnp.sum(p.reshape(tk // 8, 8, rows), axis=0)
            pt_bufs[s % 2][...] = p.astype(BF16)
            m = m_new
        cmax = None
        if s + 1 < n_chunks:
            cmax = scores_t(s + 1, queries(q_refs), st_bufs[(s + 1) % 2])
        elif s + 1 == n_chunks:
            cm_ref[...] = scores_t(0, queries(qn_refs), st_bufs[0])
        if s >= 1:
            pv = jnp.dot(vt_ref[:, (s - 1) * tk:s * tk], pt_bufs[(s - 1) % 2][...], preferred_element_type=F32)
            acc = alpha_prev * acc + pv
        alpha_prev = alpha
    out = (acc / jnp.sum(l8, axis=0, keepdims=True)).T
    for g in range(group):
        o_ref[:, g * LANES:(g + 1) * LANES] = out[g * tq:(g + 1) * tq, :].astype(o_ref.dtype)


def _dense_attention(q_list, k_list, v, n_seq, seq_len, n_kv_heads, group, name):
    n_tok = n_seq * seq_len
    tq = min(512 // group, seq_len)
    tk = min(2048, seq_len // 2)
    assert seq_len % (2 * tk) == 0 and seq_len % tq == 0
    q_blocks = seq_len // tq
    gw = group * LANES

    q_spec = pl.BlockSpec((tq, gw), lambda b, h, i: (b * q_blocks + i, h))
    qn_spec = pl.BlockSpec((tq, gw), lambda b, h, i: (b * q_blocks + jnp.minimum(i + 1, q_blocks - 1), h))
    in_specs = [q_spec for _ in q_list] + [qn_spec for _ in q_list]
    for k in k_list:
        if k.shape[1] == LANES:
            in_specs.append(pl.BlockSpec((seq_len, LANES), lambda b, h, i: (b, 0)))
        else:
            in_specs.append(pl.BlockSpec((seq_len, LANES), lambda b, h, i: (b, h)))
    in_specs.append(pl.BlockSpec((seq_len, LANES), lambda b, h, i: (b, h)))
    rows = group * tq
    est = (4 * len(q_list) * tq * gw * 2 + 2 * (len(k_list) + 1) * seq_len * LANES * 2 + 2 * tq * gw * 2
           + seq_len * LANES * 2 + 2 * rows * tk * (4 + 2) + 2 * rows * tk * 4 + 4 * rows * LANES * 4)
    kern = functools.partial(_dense_attn_kernel, n_q=len(q_list), n_k=len(k_list), group=group, tk=tk,
                             seq_len=seq_len)
    return pl.pallas_call(
        kern,
        grid=(n_seq, n_kv_heads, q_blocks),
        in_specs=in_specs,
        out_specs=q_spec,
        out_shape=jax.ShapeDtypeStruct((n_tok, n_kv_heads * gw), BF16),
        scratch_shapes=[pltpu.VMEM((LANES, seq_len), BF16),
                        pltpu.VMEM((tk, rows), F32), pltpu.VMEM((tk, rows), F32),
                        pltpu.VMEM((tk, rows), BF16), pltpu.VMEM((tk, rows), BF16),
                        pltpu.VMEM((1, rows), F32)],
        compiler_params=pltpu.CompilerParams(dimension_semantics=("parallel", "parallel", "arbitrary"),
                                             vmem_limit_bytes=_vmem_limit(est)),
        name=name,
    )(*q_list, *q_list, *k_list, v)


def _banded_attn_kernel(sink_ref, q_ref, k_ref, v_ref, o_ref, vt_ref, *, group, qb, window, seq_len, unroll):
    tq = o_ref.shape[0]
    n_sub = tq // qb
    span = qb + 2 * window
    h = pl.program_id(1)
    i = pl.program_id(2)
    vt_chunk = min(512, seq_len)

    @pl.when(i == 0)
    def _():
        def tbody(c, carry):
            start = pl.multiple_of(c * vt_chunk, vt_chunk)
            vt_ref[:, pl.ds(start, vt_chunk)] = v_ref[pl.ds(start, vt_chunk), :].T
            return carry
        lax.fori_loop(0, seq_len // vt_chunk, tbody, 0)

    sink_row = jnp.concatenate(
        [jnp.full((1, qb), sink_ref[h * group + g] * LOG2E, F32) for g in range(group)], axis=1)
    rel0 = (lax.broadcasted_iota(jnp.int32, (span, qb), 0) - lax.broadcasted_iota(jnp.int32, (span, qb), 1))

    def one_block(sb):
        q0 = pl.multiple_of(sb * qb, qb)
        q_abs = i * tq + q0
        k0 = pl.multiple_of(jnp.clip(q_abs - window, 0, seq_len - span), LANES)
        rel = rel0 + (k0 - q_abs)
        bias = jnp.where(jnp.abs(rel) <= window, 0.0, NEG).astype(F32)
        q = jnp.concatenate([q_ref[pl.ds(q0, qb), g * LANES:(g + 1) * LANES] for g in range(group)], axis=0)
        k = k_ref[pl.ds(k0, span), :]
        st = lax.dot_general(k, q, (((1,), (1,)), ((), ())), preferred_element_type=F32)
        st = st + jnp.concatenate([bias] * group, axis=1)
        m = jnp.maximum(jnp.max(st, axis=0, keepdims=True), sink_row)
        pt = jnp.exp2(st - m)
        denom = jnp.sum(pt, axis=0, keepdims=True) + jnp.exp2(sink_row - m)
        out_t = jnp.dot(vt_ref[:, pl.ds(k0, span)], pt.astype(BF16), preferred_element_type=F32) / denom
        out = out_t.T
        for g in range(group):
            o_ref[pl.ds(q0, qb), g * LANES:(g + 1) * LANES] = out[g * qb:(g + 1) * qb, :].astype(o_ref.dtype)

    def body(sbb, carry):
        for u in range(unroll):
            one_block(sbb * unroll + u)
        return carry

    lax.fori_loop(0, n_sub // unroll, body, 0)


def _banded_attention(q, k, v, sink, n_seq, seq_len, name):
    n_tok = n_seq * seq_len
    group = B_HEADS // B_KV_HEADS
    gw = group * LANES
    qb, window = 128, B_WINDOW
    tq = min(1024, seq_len)
    q_blocks = seq_len // tq
    q_spec = pl.BlockSpec((tq, gw), lambda b, h, i: (b * q_blocks + i, h))
    kv_spec = pl.BlockSpec((seq_len, LANES), lambda b, h, i: (b, h))
    span = qb + 2 * window
    unroll = 2
    assert seq_len >= span and (tq // qb) % unroll == 0
    est = 4 * tq * gw * 2 + 5 * seq_len * LANES * 2 + 6 * unroll * group * qb * span * 4
    kern = functools.partial(_banded_attn_kernel, group=group, qb=qb, window=window, seq_len=seq_len,
                             unroll=unroll)
    return pl.pallas_call(
        kern,
        grid=(n_seq, B_KV_HEADS, q_blocks),
        in_specs=[pl.BlockSpec(memory_space=pltpu.SMEM), q_spec, kv_spec, kv_spec],
        out_specs=q_spec,
        out_shape=jax.ShapeDtypeStruct((n_tok, B_HEADS * LANES), BF16),
        scratch_shapes=[pltpu.VMEM((LANES, seq_len), BF16)],
        compiler_params=pltpu.CompilerParams(dimension_semantics=("parallel", "parallel", "arbitrary"),
                                             vmem_limit_bytes=_vmem_limit(est)),
        name=name,
    )(sink.astype(F32), q, k, v)


def _out_kernel(*refs, final):
    if final:
        x_ref, o_ref, g_ref, w_ref, fn_ref, y_ref = refs
    else:
        x_ref, o_ref, g_ref, w_ref, y_ref = refs
    g = g_ref[...].astype(F32)
    u = (o_ref[...].astype(F32) * (g / (1.0 + jnp.exp(-g)))).astype(BF16)
    y = x_ref[...] + jnp.dot(u, w_ref[...], preferred_element_type=F32)
    if final:
        ms = jnp.mean(y * y, axis=-1, keepdims=True)
        y = y * lax.rsqrt(ms + EPS) * fn_ref[...]
    y_ref[...] = y


def _output(x, o, gate_arr, gate_col_block, w_o, final_norm=None, name="out"):
    n_tok, d = x.shape
    width = o.shape[1]
    tm = min(512, n_tok)
    final = final_norm is not None
    in_specs = [
        pl.BlockSpec((tm, d), lambda i: (i, 0)),
        pl.BlockSpec((tm, width), lambda i: (i, 0)),
        pl.BlockSpec((tm, width), lambda i: (i, gate_col_block)),
        pl.BlockSpec((width, d), lambda i: (0, 0), pipeline_mode=pl.Buffered(1)),
    ]
    args = [x, o, gate_arr, w_o]
    if final:
        in_specs.append(pl.BlockSpec((1, d), lambda i: (0, 0)))
        args.append(final_norm.reshape(1, d).astype(F32))
    est = (width * d * 2 + 4 * tm * d * 4 + 4 * tm * width * 2 + 3 * tm * width * 4 + 2 * tm * d * 4)
    return pl.pallas_call(
        functools.partial(_out_kernel, final=final),
        grid=(n_tok // tm,),
        in_specs=in_specs,
        out_specs=pl.BlockSpec((tm, d), lambda i: (i, 0)),
        out_shape=jax.ShapeDtypeStruct((n_tok, d), F32),
        compiler_params=pltpu.CompilerParams(dimension_semantics=("parallel",),
                                             vmem_limit_bytes=_vmem_limit(est)),
        name=name,
    )(*args)


def _angles(pos, dim, theta):
    inv = theta ** (-jnp.arange(dim // 2, dtype=F32) * (2.0 / dim))
    ang = pos.astype(F32)[:, None] * inv[None, :]
    return jnp.cos(ang), jnp.sin(ang)


def _tables_a(seq_len, scale):
    c, s = _angles(jnp.arange(seq_len), A_ROPE, A_THETA)
    z32, z64 = jnp.zeros_like(c), jnp.zeros((seq_len, 64), F32)
    cos = jnp.concatenate([c, c, z64], axis=1)
    s1 = jnp.concatenate([z32, s, z64], axis=1)
    s2 = jnp.concatenate([-s, z32, z64], axis=1)
    return cos * scale, s1 * scale, s2 * scale


def _tables_b(seq_len, scale):
    c, s = _angles(jnp.arange(seq_len), B_ROT, B_THETA)
    z16, one, z96 = jnp.zeros_like(c), jnp.ones((seq_len, 96), F32), jnp.zeros((seq_len, 96), F32)
    cos = jnp.concatenate([c, c, one], axis=1)
    s1 = jnp.concatenate([z16, s, z96], axis=1)
    s2 = jnp.concatenate([-s, z16, z96], axis=1)
    return cos * scale, s1 * scale, s2 * scale


def _tables_c(seq_len):
    t = jnp.arange(seq_len)
    half = C_HEAD_DIM // 2
    cr, sr = _angles(t // GRID_W, half, C_THETA)
    cc, sc = _angles(t % GRID_W, half, C_THETA)
    z = jnp.zeros_like(cr)
    cos = jnp.concatenate([cr, cr, cc, cc], axis=1)
    s1 = jnp.concatenate([z, sr, z, sc], axis=1)
    s2 = jnp.concatenate([-sr, z, -sc, z], axis=1)
    return cos, s1, s2


def _layer_a(x, n_seq, seq_len, norm_w, w_in, q_norm, w_uq, kv_norm, w_ukv, w_o, final_norm, tag):
    d = x.shape[1]
    width = A_HEADS * A_V
    r0 = A_Q_RANK + A_KV_RANK
    w1 = jnp.concatenate([w_in[:, r0 + A_ROPE:], w_in[:, :r0], w_in[:, r0:r0 + A_ROPE],
                          jnp.zeros((d, LANES - A_ROPE), w_in.dtype)], axis=1).astype(BF16)
    n_plain = width + r0
    segs = [Seg("plain", 0, n_plain, 0, 0), Seg("rope", n_plain, LANES, 1, 0)]
    main, k_rope = _projection(x, 0, d, norm_w, w1, segs, [n_plain, LANES], seq_len,
                               tables=_tables_a(seq_len, 1.0), roll_by=A_ROPE // 2, name=f"a_in_{tag}")
    cq_block, ckv_block = width // A_Q_RANK, (width + A_Q_RANK) // A_KV_RANK

    scale = (A_NOPE + A_ROPE) ** -0.5 * LOG2E
    wq = w_uq.reshape(A_Q_RANK, A_HEADS, A_NOPE + A_ROPE)
    wq_rope = jnp.concatenate([wq[:, :, A_NOPE:], jnp.zeros((A_Q_RANK, A_HEADS, LANES - A_ROPE), wq.dtype)], axis=2)
    wq2 = jnp.concatenate([wq[:, :, :A_NOPE].reshape(A_Q_RANK, -1), wq_rope.reshape(A_Q_RANK, -1)],
                          axis=1).astype(BF16)
    hn = A_HEADS * LANES
    segs = [Seg("plain", 0, hn, 0, 0, scale=scale), Seg("rope", hn, hn, 1, 0)]
    q_nope, q_rope = _projection(main, cq_block, A_Q_RANK, q_norm, wq2, segs, [hn, hn], seq_len,
                                 tables=_tables_a(seq_len, scale), roll_by=A_ROPE // 2, name=f"a_q_{tag}")

    wkv = w_ukv.reshape(A_KV_RANK, A_HEADS, A_NOPE + A_V)
    wkv2 = jnp.concatenate([wkv[:, :, :A_NOPE].reshape(A_KV_RANK, -1), wkv[:, :, A_NOPE:].reshape(A_KV_RANK, -1)],
                           axis=1).astype(BF16)
    segs = [Seg("plain", 0, hn, 0, 0), Seg("plain", hn, hn, 1, 0)]
    k_nope, v = _projection(main, ckv_block, A_KV_RANK, kv_norm, wkv2, segs, [hn, hn], seq_len, name=f"a_kv_{tag}")

    o = _dense_attention([q_nope, q_rope], [k_nope, k_rope], v, n_seq, seq_len, A_HEADS, 1, name=f"a_attn_{tag}")
    return _output(x, o, main, 0, w_o.astype(BF16), final_norm, name=f"a_out_{tag}")


def _gqa_projection(x, seq_len, norm_w, w_in, heads, kv_heads, kind, tables, gains, q_scale, roll_by, name):
    d = x.shape[1]
    qw, kw = heads * LANES, kv_heads * LANES
    segs = [Seg(kind, 0, qw, 0, 0, scale=q_scale, gain_idx=0), Seg(kind, qw, kw, 1, 0, gain_idx=1),
            Seg("plain", qw + kw, kw, 2, 0), Seg("plain", qw + 2 * kw, qw, 3, 0)]
    return _projection(x, 0, d, norm_w, w_in.astype(BF16), segs, [qw, kw, kw, qw], seq_len,
                       tables=tables, gains=gains, roll_by=roll_by, name=name)


def _layer_b(x, n_seq, seq_len, norm_w, w_in, sink, w_o, final_norm, tag):
    scale = B_HEAD_DIM ** -0.5 * LOG2E
    tq = _tables_b(seq_len, 1.0)
    q, k, v, gate = _gqa_projection(x, seq_len, norm_w, w_in, B_HEADS, B_KV_HEADS, "rope", tq, (), scale,
                                    B_ROT // 2, f"b_in_{tag}")
    o = _banded_attention(q, k, v, sink, n_seq, seq_len, name=f"b_attn_{tag}")
    return _output(x, o, gate, 0, w_o.astype(BF16), final_norm, name=f"b_out_{tag}")


def _layer_c(x, n_seq, seq_len, norm_w, w_in, q_gain, k_gain, w_o, final_norm, tag):
    scale = C_HEAD_DIM ** -0.5 * LOG2E
    gains = (q_gain.reshape(1, LANES).astype(F32), k_gain.reshape(1, LANES).astype(F32))
    q, k, v, gate = _gqa_projection(x, seq_len, norm_w, w_in, C_HEADS, C_KV_HEADS, "hnorm_rope",
                                    _tables_c(seq_len), gains, scale, C_HEAD_DIM // 4, f"c_in_{tag}")
    o = _dense_attention([q], [k], v, n_seq, seq_len, C_KV_HEADS, C_HEADS // C_KV_HEADS, name=f"c_attn_{tag}")
    return _output(x, o, gate, 0, w_o.astype(BF16), final_norm, name=f"c_out_{tag}")


def _trunk(x3, tag, norm_w, a_w_in, a_q_norm, a_w_uq, a_kv_norm, a_w_ukv, a_w_o, b_w_in, b_sink, b_w_o,
           c_w_in, c_q_scale, c_k_scale, c_w_o, final_norm):
    n_seq, seq_len, d = x3.shape
    depth = norm_w.shape[0]
    x = x3.reshape(n_seq * seq_len, d)
    for i in range(depth):
        kind, j = i % 3, i // 3
        fn = final_norm if i == depth - 1 else None
        if kind == 0:
            x = _layer_a(x, n_seq, seq_len, norm_w[i], a_w_in[j], a_q_norm[j], a_w_uq[j], a_kv_norm[j],
                         a_w_ukv[j], a_w_o[j], fn, f"{tag}{i}")
        elif kind == 1:
            x = _layer_b(x, n_seq, seq_len, norm_w[i], b_w_in[j], b_sink[j], b_w_o[j], fn, f"{tag}{i}")
        else:
            x = _layer_c(x, n_seq, seq_len, norm_w[i], c_w_in[j], c_q_scale[j], c_k_scale[j], c_w_o[j], fn,
                         f"{tag}{i}")
    return x.reshape(n_seq, seq_len, d)


def kernel(x_prompt, x_sample, norm_w, a_w_in, a_q_norm, a_w_uq, a_kv_norm, a_w_ukv, a_w_o, b_w_in, b_sink,
           b_w_o, c_w_in, c_q_scale, c_k_scale, c_w_o, final_norm):
    params = (norm_w, a_w_in, a_q_norm, a_w_uq, a_kv_norm, a_w_ukv, a_w_o, b_w_in, b_sink, b_w_o,
              c_w_in, c_q_scale, c_k_scale, c_w_o, final_norm)
    return (_trunk(x_prompt, "p", *params), _trunk(x_sample, "s", *params))
```

```python
import dataclasses
import functools
import math

import jax
import jax.numpy as jnp
from jax import lax
from jax.experimental import pallas as pl
from jax.experimental.pallas import tpu as pltpu

F32 = jnp.float32
BF16 = jnp.bfloat16

LANES = 128
V7X_VMEM_BYTES = 64 * 1024 * 1024
VMEM_CEILING = V7X_VMEM_BYTES - 8 * 1024 * 1024

EPS = 1e-6
NEG = -1e30
LOG2E = math.log2(math.e)
A_HEADS, A_Q_RANK, A_KV_RANK, A_NOPE, A_ROPE, A_V = 16, 512, 512, 128, 64, 128
A_THETA = 10000.0
B_HEADS, B_KV_HEADS, B_HEAD_DIM, B_WINDOW, B_ROT = 16, 4, 128, 128, 32
B_THETA = 500000.0
C_HEADS, C_KV_HEADS, C_HEAD_DIM, GRID_W = 16, 4, 128, 64
C_THETA = 10000.0

DOT_COLS = 512


def _vmem_limit(estimate_bytes):
    return int(min(max(estimate_bytes, 16 * 1024 * 1024), VMEM_CEILING))


@dataclasses.dataclass(frozen=True)
class Seg:
    kind: str
    w_col: int
    width: int
    out_idx: int
    out_col: int
    scale: float = 1.0
    gain_idx: int = -1


def _proj_kernel(*refs, segs, n_tables, n_gains, n_out, roll_by):
    x_ref, gamma_ref, w_ref = refs[:3]
    pos = 3
    tables = refs[pos:pos + n_tables]
    pos += n_tables
    gains = refs[pos:pos + n_gains]
    pos += n_gains
    outs = refs[pos:pos + n_out]
    xn_ref = refs[pos + n_out]

    x = x_ref[...].astype(F32)
    ms = jnp.mean(x * x, axis=-1, keepdims=True)
    xn_ref[...] = (x * lax.rsqrt(ms + EPS) * gamma_ref[...]).astype(BF16)

    for seg in segs:
        out_ref = outs[seg.out_idx]
        for c0 in range(0, seg.width, DOT_COLS):
            w = min(DOT_COLS, seg.width - c0)
            acc = jnp.dot(xn_ref[...], w_ref[:, seg.w_col + c0:seg.w_col + c0 + w],
                          preferred_element_type=F32)
            if seg.kind == "plain":
                if seg.scale != 1.0:
                    acc = acc * seg.scale
                out_ref[:, seg.out_col + c0:seg.out_col + c0 + w] = acc.astype(out_ref.dtype)
                continue
            for l0 in range(0, w, LANES):
                a = acc[:, l0:l0 + LANES]
                if seg.kind == "hnorm_rope":
                    hms = jnp.mean(a * a, axis=-1, keepdims=True)
                    a = a * lax.rsqrt(hms + EPS) * gains[seg.gain_idx][...]
                if len(tables) == 2:
                    a = a * tables[0][...] + pltpu.roll(a, LANES // 2, 1) * tables[1][...]
                else:
                    a = (a * tables[0][...] + pltpu.roll(a, roll_by, 1) * tables[1][...]
                         + pltpu.roll(a, LANES - roll_by, 1) * tables[2][...])
                if seg.scale != 1.0:
                    a = a * seg.scale
                o0 = seg.out_col + c0 + l0
                out_ref[:, o0:o0 + LANES] = a.astype(out_ref.dtype)


def _proj_tile(n_tok, k_dim, w_cols, out_cols, x_bytes):
    for tm in (512, 256, 128):
        if n_tok % tm:
            continue
        est = (k_dim * w_cols * 2 + 2 * tm * k_dim * x_bytes + 2 * tm * out_cols * 2
               + tm * k_dim * 2 + 3 * tm * DOT_COLS * 4 + 6 * tm * LANES * 4)
        if est <= VMEM_CEILING - 4 * 1024 * 1024:
            return tm, est
    raise ValueError("projection does not fit VMEM")


def _projection(x, x_col_block, k_dim, gamma, w, segs, out_widths, seq_len, tables=(), gains=(),
                roll_by=0, name="proj"):
    n_tok = x.shape[0]
    w_cols = w.shape[1]
    tm, est = _proj_tile(n_tok, k_dim, w_cols, sum(out_widths), x.dtype.itemsize)
    tm = min(tm, seq_len)
    pos_blocks = seq_len // tm
    in_specs = [
        pl.BlockSpec((tm, k_dim), lambda i: (i, x_col_block)),
        pl.BlockSpec((1, k_dim), lambda i: (0, 0)),
        pl.BlockSpec((k_dim, w_cols), lambda i: (0, 0), pipeline_mode=pl.Buffered(1)),
    ]
    in_specs += [pl.BlockSpec((tm, LANES), lambda i: (i % pos_blocks, 0)) for _ in tables]
    in_specs += [pl.BlockSpec((1, LANES), lambda i: (0, 0)) for _ in gains]
    out_specs = [pl.BlockSpec((tm, ow), lambda i: (i, 0)) for ow in out_widths]
    out_shape = [jax.ShapeDtypeStruct((n_tok, ow), BF16) for ow in out_widths]
    kern = functools.partial(_proj_kernel, segs=tuple(segs), n_tables=len(tables), n_gains=len(gains),
                             n_out=len(out_widths), roll_by=roll_by)
    return pl.pallas_call(
        kern,
        grid=(n_tok // tm,),
        in_specs=in_specs,
        out_specs=out_specs,
        out_shape=out_shape,
        scratch_shapes=[pltpu.VMEM((tm, k_dim), BF16)],
        compiler_params=pltpu.CompilerParams(dimension_semantics=("parallel",),
                                             vmem_limit_bytes=_vmem_limit(est)),
        name=name,
    )(x, gamma.reshape(1, k_dim).astype(F32), w, *tables, *gains)


SCORE_SCRATCH_BYTES = 4 * 1024 * 1024
DENSE_KEY_CHUNKS = 4
DENSE_MAX_ROWS = 1024
DENSE_ITEMS_PER_STEP = 8


def _dense_tiles(seq_len, group):
    tk = max(512, min(2048, seq_len // DENSE_KEY_CHUNKS))
    tk = min(tk, seq_len // 2)
    rows = min(DENSE_MAX_ROWS, SCORE_SCRATCH_BYTES // (4 * tk), seq_len * group)
    n_chunks = seq_len // tk
    q_per_step = max(1, min(DENSE_ITEMS_PER_STEP // n_chunks, seq_len * group // rows))
    if (q_per_step * n_chunks) % 2:
        q_per_step = 1
    return tk, rows, q_per_step


def _dense_attn_kernel(*refs, n_q, n_k, group, tk, seq_len, q_per_step):
    q_refs = refs[:n_q]
    qn_refs = refs[n_q:2 * n_q]
    k_refs = refs[2 * n_q:2 * n_q + n_k]
    v_ref = refs[2 * n_q + n_k]
    o_ref = refs[2 * n_q + n_k + 1]
    vt_ref, st0, st1, pt0, pt1, cm_ref = refs[2 * n_q + n_k + 2:]
    st_bufs, pt_bufs = (st0, st1), (pt0, pt1)
    tq = o_ref.shape[0] // q_per_step
    rows = group * tq
    n_chunks = seq_len // tk
    n_items = q_per_step * n_chunks
    assert n_items % 2 == 0

    @pl.when(pl.program_id(2) == 0)
    def _():
        def tbody(c, carry):
            start = pl.multiple_of(c * tk, tk)
            vt_ref[:, pl.ds(start, tk)] = v_ref[pl.ds(start, tk), :].T
            return carry
        lax.fori_loop(0, n_chunks, tbody, 0)

    def queries(q_list, blk):
        def stack_heads(ref):
            return jnp.concatenate([ref[blk * tq:(blk + 1) * tq, g * LANES:(g + 1) * LANES]
                                    for g in range(group)], axis=0)
        return jnp.concatenate([stack_heads(r) for r in q_list], axis=1)

    def scores_t(j, q, dst):
        k = jnp.concatenate([r[j * tk:(j + 1) * tk, :] for r in k_refs], axis=1)
        st = lax.dot_general(k, q, (((1,), (1,)), ((), ())), preferred_element_type=F32)
        dst[...] = st
        return jnp.max(st, axis=0, keepdims=True)

    @pl.when(pl.program_id(2) == 0)
    def _():
        cm_ref[...] = scores_t(0, queries(q_refs, 0), st0)

    m = l8 = acc = None
    cmax = cm_ref[...]
    alpha_prev = None
    for t in range(n_items + 1):
        alpha = None
        if t < n_items:
            blk, c = divmod(t, n_chunks)
            if c == 0:
                m = jnp.full((1, rows), -jnp.inf, F32)
                l8 = jnp.zeros((8, rows), F32)
            m_new = jnp.maximum(m, cmax)
            alpha = jnp.exp2(m - m_new)
            p = jnp.exp2(st_bufs[t % 2][...] - m_new)
            l8 = alpha * l8 + jnp.sum(p.reshape(tk // 8, 8, rows), axis=0)
            pt_bufs[t % 2][...] = p.astype(BF16)
            m = m_new
        cmax = None
        if t + 1 < n_items:
            blk, c = divmod(t + 1, n_chunks)
            cmax = scores_t(c, queries(q_refs, blk), st_bufs[(t + 1) % 2])
        elif t + 1 == n_items:
            cm_ref[...] = scores_t(0, queries(qn_refs, 0), st_bufs[0])
        if t >= 1:
            blk, c = divmod(t - 1, n_chunks)
            pv = jnp.dot(vt_ref[:, c * tk:(c + 1) * tk], pt_bufs[(t - 1) % 2][...], preferred_element_type=F32)
            acc = pv if c == 0 else alpha_prev * acc + pv
            if c == n_chunks - 1:
                out = (acc / jnp.sum(l8_done, axis=0, keepdims=True)).T
                for g in range(group):
                    o_ref[blk * tq:(blk + 1) * tq, g * LANES:(g + 1) * LANES] = (
                        out[g * tq:(g + 1) * tq, :].astype(o_ref.dtype))
        alpha_prev = alpha
        if t < n_items and (t + 1) % n_chunks == 0:
            l8_done = l8


def _dense_attention(q_list, k_list, v, n_seq, seq_len, n_kv_heads, group, name):
    n_tok = n_seq * seq_len
    tk, rows, q_per_step = _dense_tiles(seq_len, group)
    tq = rows // group
    tq_step = q_per_step * tq
    assert seq_len % tk == 0 and seq_len % tq_step == 0
    q_blocks = seq_len // tq
    q_steps = seq_len // tq_step
    gw = group * LANES

    q_spec = pl.BlockSpec((tq_step, gw), lambda b, h, i: (b * q_steps + i, h))
    qn_spec = pl.BlockSpec(
        (tq, gw), lambda b, h, i: (b * q_blocks + jnp.minimum((i + 1) * q_per_step, q_blocks - 1), h))
    in_specs = [q_spec for _ in q_list] + [qn_spec for _ in q_list]
    for k in k_list:
        if k.shape[1] == LANES:
            in_specs.append(pl.BlockSpec((seq_len, LANES), lambda b, h, i: (b, 0)))
        else:
            in_specs.append(pl.BlockSpec((seq_len, LANES), lambda b, h, i: (b, h)))
    in_specs.append(pl.BlockSpec((seq_len, LANES), lambda b, h, i: (b, h)))
    est = (4 * len(q_list) * tq_step * gw * 2 + 2 * (len(k_list) + 1) * seq_len * LANES * 2
           + 2 * tq_step * gw * 2 + seq_len * LANES * 2 + 2 * rows * tk * (4 + 2) + 2 * rows * tk * 4
           + 4 * rows * LANES * 4)
    kern = functools.partial(_dense_attn_kernel, n_q=len(q_list), n_k=len(k_list), group=group, tk=tk,
                             seq_len=seq_len, q_per_step=q_per_step)
    return pl.pallas_call(
        kern,
        grid=(n_seq, n_kv_heads, q_steps),
        in_specs=in_specs,
        out_specs=q_spec,
        out_shape=jax.ShapeDtypeStruct((n_tok, n_kv_heads * gw), BF16),
        scratch_shapes=[pltpu.VMEM((LANES, seq_len), BF16),
                        pltpu.VMEM((tk, rows), F32), pltpu.VMEM((tk, rows), F32),
                        pltpu.VMEM((tk, rows), BF16), pltpu.VMEM((tk, rows), BF16),
                        pltpu.VMEM((1, rows), F32)],
        compiler_params=pltpu.CompilerParams(dimension_semantics=("parallel", "parallel", "arbitrary"),
                                             vmem_limit_bytes=_vmem_limit(est)),
        name=name,
    )(*q_list, *q_list, *k_list, v)


def _banded_attn_kernel(sink_ref, q_ref, k_ref, v_ref, o_ref, vt_ref, *, group, qb, window, seq_len, unroll):
    tq = o_ref.shape[0]
    n_sub = tq // qb
    span = qb + 2 * window
    h = pl.program_id(1)
    i = pl.program_id(2)
    vt_chunk = min(512, seq_len)

    @pl.when(i == 0)
    def _():
        def tbody(c, carry):
            start = pl.multiple_of(c * vt_chunk, vt_chunk)
            vt_ref[:, pl.ds(start, vt_chunk)] = v_ref[pl.ds(start, vt_chunk), :].T
            return carry
        lax.fori_loop(0, seq_len // vt_chunk, tbody, 0)

    sink_row = jnp.concatenate(
        [jnp.full((1, qb), sink_ref[h * group + g] * LOG2E, F32) for g in range(group)], axis=1)
    rel0 = (lax.broadcasted_iota(jnp.int32, (span, qb), 0) - lax.broadcasted_iota(jnp.int32, (span, qb), 1))

    def one_block(sb):
        q0 = pl.multiple_of(sb * qb, qb)
        q_abs = i * tq + q0
        k0 = pl.multiple_of(jnp.clip(q_abs - window, 0, seq_len - span), LANES)
        rel = rel0 + (k0 - q_abs)
        bias = jnp.where(jnp.abs(rel) <= window, 0.0, NEG).astype(F32)
        q = jnp.concatenate([q_ref[pl.ds(q0, qb), g * LANES:(g + 1) * LANES] for g in range(group)], axis=0)
        k = k_ref[pl.ds(k0, span), :]
        st = lax.dot_general(k, q, (((1,), (1,)), ((), ())), preferred_element_type=F32)
        st = st + jnp.concatenate([bias] * group, axis=1)
        m = jnp.maximum(jnp.max(st, axis=0, keepdims=True), sink_row)
        pt = jnp.exp2(st - m)
        denom = jnp.sum(pt, axis=0, keepdims=True) + jnp.exp2(sink_row - m)
        out_t = jnp.dot(vt_ref[:, pl.ds(k0, span)], pt.astype(BF16), preferred_element_type=F32) / denom
        out = out_t.T
        for g in range(group):
            o_ref[pl.ds(q0, qb), g * LANES:(g + 1) * LANES] = out[g * qb:(g + 1) * qb, :].astype(o_ref.dtype)

    def body(sbb, carry):
        for u in range(unroll):
            one_block(sbb * unroll + u)
        return carry

    lax.fori_loop(0, n_sub // unroll, body, 0)


def _banded_attention(q, k, v, sink, n_seq, seq_len, name):
    n_tok = n_seq * seq_len
    group = B_HEADS // B_KV_HEADS
    gw = group * LANES
    qb, window = 128, B_WINDOW
    tq = min(1024, seq_len)
    q_blocks = seq_len // tq
    q_spec = pl.BlockSpec((tq, gw), lambda b, h, i: (b * q_blocks + i, h))
    kv_spec = pl.BlockSpec((seq_len, LANES), lambda b, h, i: (b, h))
    span = qb + 2 * window
    unroll = min(4, tq // qb)
    assert seq_len >= span and (tq // qb) % unroll == 0
    est = 4 * tq * gw * 2 + 5 * seq_len * LANES * 2 + 6 * unroll * group * qb * span * 4
    kern = functools.partial(_banded_attn_kernel, group=group, qb=qb, window=window, seq_len=seq_len,
                             unroll=unroll)
    return pl.pallas_call(
        kern,
        grid=(n_seq, B_KV_HEADS, q_blocks),
        in_specs=[pl.BlockSpec(memory_space=pltpu.SMEM), q_spec, kv_spec, kv_spec],
        out_specs=q_spec,
        out_shape=jax.ShapeDtypeStruct((n_tok, B_HEADS * LANES), BF16),
        scratch_shapes=[pltpu.VMEM((LANES, seq_len), BF16)],
        compiler_params=pltpu.CompilerParams(dimension_semantics=("parallel", "parallel", "arbitrary"),
                                             vmem_limit_bytes=_vmem_limit(est)),
        name=name,
    )(sink.astype(F32), q, k, v)


def _out_kernel(*refs, final):
    if final:
        x_ref, o_ref, g_ref, w_ref, fn_ref, y_ref = refs
    else:
        x_ref, o_ref, g_ref, w_ref, y_ref = refs
    g = g_ref[...].astype(F32)
    u = (o_ref[...].astype(F32) * (g / (1.0 + jnp.exp(-g)))).astype(BF16)
    y = x_ref[...] + jnp.dot(u, w_ref[...], preferred_element_type=F32)
    if final:
        ms = jnp.mean(y * y, axis=-1, keepdims=True)
        y = y * lax.rsqrt(ms + EPS) * fn_ref[...]
    y_ref[...] = y


def _output(x, o, gate_arr, gate_col_block, w_o, final_norm=None, name="out"):
    n_tok, d = x.shape
    width = o.shape[1]
    tm = min(512, n_tok)
    final = final_norm is not None
    in_specs = [
        pl.BlockSpec((tm, d), lambda i: (i, 0)),
        pl.BlockSpec((tm, width), lambda i: (i, 0)),
        pl.BlockSpec((tm, width), lambda i: (i, gate_col_block)),
        pl.BlockSpec((width, d), lambda i: (0, 0), pipeline_mode=pl.Buffered(1)),
    ]
    args = [x, o, gate_arr, w_o]
    if final:
        in_specs.append(pl.BlockSpec((1, d), lambda i: (0, 0)))
        args.append(final_norm.reshape(1, d).astype(F32))
    est = (width * d * 2 + 4 * tm * d * 4 + 4 * tm * width * 2 + 3 * tm * width * 4 + 2 * tm * d * 4)
    return pl.pallas_call(
        functools.partial(_out_kernel, final=final),
        grid=(n_tok // tm,),
        in_specs=in_specs,
        out_specs=pl.BlockSpec((tm, d), lambda i: (i, 0)),
        out_shape=jax.ShapeDtypeStruct((n_tok, d), F32),
        compiler_params=pltpu.CompilerParams(dimension_semantics=("parallel",),
                                             vmem_limit_bytes=_vmem_limit(est)),
        name=name,
    )(*args)


def _angles(pos, dim, theta):
    inv = theta ** (-jnp.arange(dim // 2, dtype=F32) * (2.0 / dim))
    ang = pos.astype(F32)[:, None] * inv[None, :]
    return jnp.cos(ang), jnp.sin(ang)


def _tables_a(seq_len, scale):
    c, s = _angles(jnp.arange(seq_len), A_ROPE, A_THETA)
    z = jnp.zeros_like(c)
    cos = jnp.concatenate([c, z, c, z], axis=1)
    sin_signed = jnp.concatenate([-s, z, s, z], axis=1)
    return cos * scale, sin_signed * scale


def _tables_b(seq_len, scale):
    c, s = _angles(jnp.arange(seq_len), B_ROT, B_THETA)
    z16, one, z96 = jnp.zeros_like(c), jnp.ones((seq_len, 96), F32), jnp.zeros((seq_len, 96), F32)
    cos = jnp.concatenate([c, c, one], axis=1)
    s1 = jnp.concatenate([z16, s, z96], axis=1)
    s2 = jnp.concatenate([-s, z16, z96], axis=1)
    return cos * scale, s1 * scale, s2 * scale


def _tables_c(seq_len):
    t = jnp.arange(seq_len)
    half = C_HEAD_DIM // 2
    cr, sr = _angles(t // GRID_W, half, C_THETA)
    cc, sc = _angles(t % GRID_W, half, C_THETA)
    z = jnp.zeros_like(cr)
    cos = jnp.concatenate([cr, cr, cc, cc], axis=1)
    s1 = jnp.concatenate([z, sr, z, sc], axis=1)
    s2 = jnp.concatenate([-sr, z, -sc, z], axis=1)
    return cos, s1, s2


def _layer_a(x, n_seq, seq_len, norm_w, w_in, q_norm, w_uq, kv_norm, w_ukv, w_o, final_norm, tag):
    d = x.shape[1]
    width = A_HEADS * A_V
    r0 = A_Q_RANK + A_KV_RANK
    half = A_ROPE // 2
    zpad = jnp.zeros((d, LANES // 2 - half), w_in.dtype)
    w1 = jnp.concatenate([w_in[:, r0 + A_ROPE:], w_in[:, :r0], w_in[:, r0:r0 + half], zpad,
                          w_in[:, r0 + half:r0 + A_ROPE], zpad], axis=1).astype(BF16)
    n_plain = width + r0
    segs = [Seg("plain", 0, n_plain, 0, 0), Seg("rope", n_plain, LANES, 1, 0)]
    main, k_rope = _projection(x, 0, d, norm_w, w1, segs, [n_plain, LANES], seq_len,
                               tables=_tables_a(seq_len, 1.0), roll_by=A_ROPE // 2, name=f"a_in_{tag}")
    cq_block, ckv_block = width // A_Q_RANK, (width + A_Q_RANK) // A_KV_RANK

    scale = (A_NOPE + A_ROPE) ** -0.5 * LOG2E
    wq = w_uq.reshape(A_Q_RANK, A_HEADS, A_NOPE + A_ROPE)
    zq = jnp.zeros((A_Q_RANK, A_HEADS, LANES // 2 - half), wq.dtype)
    wq_rope = jnp.concatenate([wq[:, :, A_NOPE:A_NOPE + half], zq, wq[:, :, A_NOPE + half:], zq], axis=2)
    wq2 = jnp.concatenate([wq[:, :, :A_NOPE].reshape(A_Q_RANK, -1), wq_rope.reshape(A_Q_RANK, -1)],
                          axis=1).astype(BF16)
    hn = A_HEADS * LANES
    segs = [Seg("plain", 0, hn, 0, 0, scale=scale), Seg("rope", hn, hn, 1, 0)]
    q_nope, q_rope = _projection(main, cq_block, A_Q_RANK, q_norm, wq2, segs, [hn, hn], seq_len,
                                 tables=_tables_a(seq_len, scale), roll_by=A_ROPE // 2, name=f"a_q_{tag}")

    wkv = w_ukv.reshape(A_KV_RANK, A_HEADS, A_NOPE + A_V)
    wkv2 = jnp.concatenate([wkv[:, :, :A_NOPE].reshape(A_KV_RANK, -1), wkv[:, :, A_NOPE:].reshape(A_KV_RANK, -1)],
                           axis=1).astype(BF16)
    segs = [Seg("plain", 0, hn, 0, 0), Seg("plain", hn, hn, 1, 0)]
    k_nope, v = _projection(main, ckv_block, A_KV_RANK, kv_norm, wkv2, segs, [hn, hn], seq_len, name=f"a_kv_{tag}")

    o = _dense_attention([q_nope, q_rope], [k_nope, k_rope], v, n_seq, seq_len, A_HEADS, 1, name=f"a_attn_{tag}")
    return _output(x, o, main, 0, w_o.astype(BF16), final_norm, name=f"a_out_{tag}")


def _gqa_projection(x, seq_len, norm_w, w_in, heads, kv_heads, kind, tables, gains, q_scale, roll_by, name):
    d = x.shape[1]
    qw, kw = heads * LANES, kv_heads * LANES
    segs = [Seg(kind, 0, qw, 0, 0, scale=q_scale, gain_idx=0), Seg(kind, qw, kw, 1, 0, gain_idx=1),
            Seg("plain", qw + kw, kw, 2, 0), Seg("plain", qw + 2 * kw, qw, 3, 0)]
    return _projection(x, 0, d, norm_w, w_in.astype(BF16), segs, [qw, kw, kw, qw], seq_len,
                       tables=tables, gains=gains, roll_by=roll_by, name=name)


def _layer_b(x, n_seq, seq_len, norm_w, w_in, sink, w_o, final_norm, tag):
    scale = B_HEAD_DIM ** -0.5 * LOG2E
    tq = _tables_b(seq_len, 1.0)
    q, k, v, gate = _gqa_projection(x, seq_len, norm_w, w_in, B_HEADS, B_KV_HEADS, "rope", tq, (), scale,
                                    B_ROT // 2, f"b_in_{tag}")
    o = _banded_attention(q, k, v, sink, n_seq, seq_len, name=f"b_attn_{tag}")
    return _output(x, o, gate, 0, w_o.astype(BF16), final_norm, name=f"b_out_{tag}")


def _layer_c(x, n_seq, seq_len, norm_w, w_in, q_gain, k_gain, w_o, final_norm, tag):
    scale = C_HEAD_DIM ** -0.5 * LOG2E
    gains = (q_gain.reshape(1, LANES).astype(F32), k_gain.reshape(1, LANES).astype(F32))
    q, k, v, gate = _gqa_projection(x, seq_len, norm_w, w_in, C_HEADS, C_KV_HEADS, "hnorm_rope",
                                    _tables_c(seq_len), gains, scale, C_HEAD_DIM // 4, f"c_in_{tag}")
    o = _dense_attention([q], [k], v, n_seq, seq_len, C_KV_HEADS, C_HEADS // C_KV_HEADS, name=f"c_attn_{tag}")
    return _output(x, o, gate, 0, w_o.astype(BF16), final_norm, name=f"c_out_{tag}")


def _trunk(x3, tag, norm_w, a_w_in, a_q_norm, a_w_uq, a_kv_norm, a_w_ukv, a_w_o, b_w_in, b_sink, b_w_o,
           c_w_in, c_q_scale, c_k_scale, c_w_o, final_norm):
    n_seq, seq_len, d = x3.shape
    depth = norm_w.shape[0]
    x = x3.reshape(n_seq * seq_len, d)
    for i in range(depth):
        kind, j = i % 3, i // 3
        fn = final_norm if i == depth - 1 else None
        if kind == 0:
            x = _layer_a(x, n_seq, seq_len, norm_w[i], a_w_in[j], a_q_norm[j], a_w_uq[j], a_kv_norm[j],
                         a_w_ukv[j], a_w_o[j], fn, f"{tag}{i}")
        elif kind == 1:
            x = _layer_b(x, n_seq, seq_len, norm_w[i], b_w_in[j], b_sink[j], b_w_o[j], fn, f"{tag}{i}")
        else:
            x = _layer_c(x, n_seq, seq_len, norm_w[i], c_w_in[j], c_q_scale[j], c_k_scale[j], c_w_o[j], fn,
                         f"{tag}{i}")
    return x.reshape(n_seq, seq_len, d)


def kernel(x_prompt, x_sample, norm_w, a_w_in, a_q_norm, a_w_uq, a_kv_norm, a_w_ukv, a_w_o, b_w_in, b_sink,
           b_w_o, c_w_in, c_q_scale, c_k_scale, c_w_o, final_norm):
    params = (norm_w, a_w_in, a_q_norm, a_w_uq, a_kv_norm, a_w_ukv, a_w_o, b_w_in, b_sink, b_w_o,
              c_w_in, c_q_scale, c_k_scale, c_w_o, final_norm)
    return (_trunk(x_prompt, "p", *params), _trunk(x_sample, "s", *params))
```

```python
import dataclasses
import functools
import math

import jax
import jax.numpy as jnp
import numpy as np
from jax import lax
from jax.experimental import pallas as pl
from jax.experimental.pallas import tpu as pltpu

F32 = jnp.float32
BF16 = jnp.bfloat16

LANES = 128
V7X_VMEM_BYTES = 64 * 1024 * 1024
VMEM_CEILING = V7X_VMEM_BYTES - 8 * 1024 * 1024

EPS = 1e-6
NEG = -1e30
LOG2E = math.log2(math.e)
A_HEADS, A_Q_RANK, A_KV_RANK, A_NOPE, A_ROPE, A_V = 16, 512, 512, 128, 64, 128
A_THETA = 10000.0
B_HEADS, B_KV_HEADS, B_HEAD_DIM, B_WINDOW, B_ROT = 16, 4, 128, 128, 32
B_THETA = 500000.0
C_HEADS, C_KV_HEADS, C_HEAD_DIM, GRID_W = 16, 4, 128, 64
C_THETA = 10000.0

DOT_COLS = 512


def _vmem_limit(estimate_bytes):
    return int(min(max(estimate_bytes, 16 * 1024 * 1024), VMEM_CEILING))


@dataclasses.dataclass(frozen=True)
class Seg:
    kind: str
    w_col: int
    width: int
    out_idx: int
    out_col: int
    scale: float = 1.0
    gain_idx: int = -1
    w_idx: int = 0


def _proj_kernel(*refs, segs, n_w, n_tables, n_gains, n_out, roll_by):
    x_ref, gamma_ref = refs[:2]
    w_refs = refs[2:2 + n_w]
    pos = 2 + n_w
    tables = refs[pos:pos + n_tables]
    pos += n_tables
    gains = refs[pos:pos + n_gains]
    pos += n_gains
    outs = refs[pos:pos + n_out]
    xn_ref = refs[pos + n_out]

    x = x_ref[...].astype(F32)
    ms = jnp.mean(x * x, axis=-1, keepdims=True)
    xn_ref[...] = (x * lax.rsqrt(ms + EPS) * gamma_ref[...]).astype(BF16)

    for seg in segs:
        out_ref = outs[seg.out_idx]
        w_ref = w_refs[seg.w_idx]
        for c0 in range(0, seg.width, DOT_COLS):
            w = min(DOT_COLS, seg.width - c0)
            acc = jnp.dot(xn_ref[...], w_ref[:, seg.w_col + c0:seg.w_col + c0 + w],
                          preferred_element_type=F32)
            if seg.kind in ("plain", "silu"):
                if seg.kind == "silu":
                    acc = acc / (1.0 + jnp.exp(-acc))
                if seg.scale != 1.0:
                    acc = acc * seg.scale
                out_ref[:, seg.out_col + c0:seg.out_col + c0 + w] = acc.astype(out_ref.dtype)
                continue
            for l0 in range(0, w, LANES):
                a = acc[:, l0:l0 + LANES]
                if seg.kind == "hnorm_rope":
                    hms = jnp.mean(a * a, axis=-1, keepdims=True)
                    a = a * lax.rsqrt(hms + EPS) * gains[seg.gain_idx][...]
                if len(tables) == 2:
                    a = a * tables[0][...] + pltpu.roll(a, LANES // 2, 1) * tables[1][...]
                else:
                    a = (a * tables[0][...] + pltpu.roll(a, roll_by, 1) * tables[1][...]
                         + pltpu.roll(a, LANES - roll_by, 1) * tables[2][...])
                if seg.scale != 1.0:
                    a = a * seg.scale
                o0 = seg.out_col + c0 + l0
                out_ref[:, o0:o0 + LANES] = a.astype(out_ref.dtype)


def _proj_tile(n_tok, k_dim, w_cols, out_cols, x_bytes):
    for tm in (512, 256, 128):
        if n_tok % tm:
            continue
        est = (k_dim * w_cols * 2 + 2 * tm * k_dim * x_bytes + 2 * tm * out_cols * 2
               + tm * k_dim * 2 + 3 * tm * DOT_COLS * 4 + 6 * tm * LANES * 4)
        if est <= VMEM_CEILING - 4 * 1024 * 1024:
            return tm, est
    raise ValueError("projection does not fit VMEM")


def _projection(x, x_col_block, k_dim, gamma, ws, segs, out_widths, seq_len, tables=(), gains=(),
                roll_by=0, name="proj"):
    n_tok = x.shape[0]
    w_cols = sum(w.shape[1] for w in ws)
    tm, est = _proj_tile(n_tok, k_dim, w_cols, sum(out_widths), x.dtype.itemsize)
    tm = min(tm, seq_len)
    pos_blocks = seq_len // tm
    in_specs = [
        pl.BlockSpec((tm, k_dim), lambda i: (i, x_col_block)),
        pl.BlockSpec((1, k_dim), lambda i: (0, 0)),
    ]
    in_specs += [pl.BlockSpec((k_dim, w.shape[1]), lambda i: (0, 0), pipeline_mode=pl.Buffered(1)) for w in ws]
    in_specs += [pl.BlockSpec((tm, LANES), lambda i: (i % pos_blocks, 0)) for _ in tables]
    in_specs += [pl.BlockSpec((1, LANES), lambda i: (0, 0)) for _ in gains]
    out_specs = [pl.BlockSpec((tm, ow), lambda i: (i, 0)) for ow in out_widths]
    out_shape = [jax.ShapeDtypeStruct((n_tok, ow), BF16) for ow in out_widths]
    kern = functools.partial(_proj_kernel, segs=tuple(segs), n_w=len(ws), n_tables=len(tables),
                             n_gains=len(gains), n_out=len(out_widths), roll_by=roll_by)
    return pl.pallas_call(
        kern,
        grid=(n_tok // tm,),
        in_specs=in_specs,
        out_specs=out_specs,
        out_shape=out_shape,
        scratch_shapes=[pltpu.VMEM((tm, k_dim), BF16)],
        compiler_params=pltpu.CompilerParams(dimension_semantics=("parallel",),
                                             vmem_limit_bytes=_vmem_limit(est)),
        name=name,
    )(x, gamma.reshape(1, k_dim).astype(F32), *ws, *tables, *gains)


SCORE_SCRATCH_BYTES = 4 * 1024 * 1024
DENSE_KEY_CHUNKS = 4
DENSE_MAX_ROWS = 1024
DENSE_ITEMS_PER_STEP = 8


def _dense_tiles(seq_len, group):
    tk = max(512, min(2048, seq_len // DENSE_KEY_CHUNKS))
    tk = min(tk, seq_len // 2)
    rows = min(DENSE_MAX_ROWS, SCORE_SCRATCH_BYTES // (4 * tk), seq_len * group)
    n_chunks = seq_len // tk
    q_blocks = seq_len * group // rows
    q_per_step = q_blocks if q_blocks * n_chunks <= DENSE_ITEMS_PER_STEP else 1
    assert (q_per_step * n_chunks) % 2 == 0
    return tk, rows, q_per_step


def _dense_attn_kernel(*refs, n_q, n_k, group, tk, seq_len, q_per_step):
    q_refs = refs[:n_q]
    qn_refs = refs[n_q:2 * n_q]
    k_refs = refs[2 * n_q:2 * n_q + n_k]
    v_ref = refs[2 * n_q + n_k]
    o_ref = refs[2 * n_q + n_k + 1]
    vt_ref, st0, st1, pt0, pt1, cm_ref = refs[2 * n_q + n_k + 2:]
    st_bufs, pt_bufs = (st0, st1), (pt0, pt1)
    tq = o_ref.shape[0] // q_per_step
    rows = group * tq
    n_chunks = seq_len // tk
    n_items = q_per_step * n_chunks
    assert n_items % 2 == 0

    @pl.when(pl.program_id(2) == 0)
    def _():
        def tbody(c, carry):
            start = pl.multiple_of(c * tk, tk)
            vt_ref[:, pl.ds(start, tk)] = v_ref[pl.ds(start, tk), :].T
            return carry
        lax.fori_loop(0, n_chunks, tbody, 0)

    def queries(q_list, blk):
        def stack_heads(ref):
            return jnp.concatenate([ref[blk * tq:(blk + 1) * tq, g * LANES:(g + 1) * LANES]
                                    for g in range(group)], axis=0)
        return jnp.concatenate([stack_heads(r) for r in q_list], axis=1)

    def scores_t(j, q, dst):
        k = jnp.concatenate([r[j * tk:(j + 1) * tk, :] for r in k_refs], axis=1)
        st = lax.dot_general(k, q, (((1,), (1,)), ((), ())), preferred_element_type=F32)
        dst[...] = st
        return jnp.max(st, axis=0, keepdims=True)

    @pl.when(pl.program_id(2) == 0)
    def _():
        cm_ref[...] = scores_t(0, queries(q_refs, 0), st0)

    m = l8 = acc = None
    cmax = cm_ref[...]
    alpha_prev = None
    for t in range(n_items + 1):
        alpha = None
        if t < n_items:
            blk, c = divmod(t, n_chunks)
            if c == 0:
                m = jnp.full((1, rows), -jnp.inf, F32)
                l8 = jnp.zeros((8, rows), F32)
            m_new = jnp.maximum(m, cmax)
            alpha = jnp.exp2(m - m_new)
            p = jnp.exp2(st_bufs[t % 2][...] - m_new)
            l8 = alpha * l8 + jnp.sum(p.reshape(tk // 8, 8, rows), axis=0)
            pt_bufs[t % 2][...] = p.astype(BF16)
            m = m_new
        cmax = None
        if t + 1 < n_items:
            blk, c = divmod(t + 1, n_chunks)
            cmax = scores_t(c, queries(q_refs, blk), st_bufs[(t + 1) % 2])
        elif t + 1 == n_items:
            cm_ref[...] = scores_t(0, queries(qn_refs, 0), st_bufs[0])
        if t >= 1:
            blk, c = divmod(t - 1, n_chunks)
            pv = jnp.dot(vt_ref[:, c * tk:(c + 1) * tk], pt_bufs[(t - 1) % 2][...], preferred_element_type=F32)
            acc = pv if c == 0 else alpha_prev * acc + pv
            if c == n_chunks - 1:
                out = (acc / jnp.sum(l8_done, axis=0, keepdims=True)).T
                for g in range(group):
                    o_ref[blk * tq:(blk + 1) * tq, g * LANES:(g + 1) * LANES] = (
                        out[g * tq:(g + 1) * tq, :].astype(o_ref.dtype))
        alpha_prev = alpha
        if t < n_items and (t + 1) % n_chunks == 0:
            l8_done = l8


def _dense_attention(q_list, k_list, v, n_seq, seq_len, n_kv_heads, group, name):
    n_tok = n_seq * seq_len
    tk, rows, q_per_step = _dense_tiles(seq_len, group)
    tq = rows // group
    tq_step = q_per_step * tq
    assert seq_len % tk == 0 and seq_len % tq_step == 0
    q_blocks = seq_len // tq
    q_steps = seq_len // tq_step
    gw = group * LANES

    q_spec = pl.BlockSpec((tq_step, gw), lambda b, h, i: (b * q_steps + i, h))
    qn_spec = pl.BlockSpec(
        (tq, gw), lambda b, h, i: (b * q_blocks + jnp.minimum((i + 1) * q_per_step, q_blocks - 1), h))
    in_specs = [q_spec for _ in q_list] + [qn_spec for _ in q_list]
    for k in k_list:
        if k.shape[1] == LANES:
            in_specs.append(pl.BlockSpec((seq_len, LANES), lambda b, h, i: (b, 0)))
        else:
            in_specs.append(pl.BlockSpec((seq_len, LANES), lambda b, h, i: (b, h)))
    in_specs.append(pl.BlockSpec((seq_len, LANES), lambda b, h, i: (b, h)))
    est = (4 * len(q_list) * tq_step * gw * 2 + 2 * (len(k_list) + 1) * seq_len * LANES * 2
           + 2 * tq_step * gw * 2 + seq_len * LANES * 2 + 2 * rows * tk * (4 + 2) + 2 * rows * tk * 4
           + 4 * rows * LANES * 4)
    kern = functools.partial(_dense_attn_kernel, n_q=len(q_list), n_k=len(k_list), group=group, tk=tk,
                             seq_len=seq_len, q_per_step=q_per_step)
    return pl.pallas_call(
        kern,
        grid=(n_seq, n_kv_heads, q_steps),
        in_specs=in_specs,
        out_specs=q_spec,
        out_shape=jax.ShapeDtypeStruct((n_tok, n_kv_heads * gw), BF16),
        scratch_shapes=[pltpu.VMEM((LANES, seq_len), BF16),
                        pltpu.VMEM((tk, rows), F32), pltpu.VMEM((tk, rows), F32),
                        pltpu.VMEM((tk, rows), BF16), pltpu.VMEM((tk, rows), BF16),
                        pltpu.VMEM((1, rows), F32)],
        compiler_params=pltpu.CompilerParams(dimension_semantics=("parallel", "parallel", "arbitrary"),
                                             vmem_limit_bytes=_vmem_limit(est)),
        name=name,
    )(*q_list, *q_list, *k_list, v)


def _banded_attn_kernel(sink_ref, q_ref, k_ref, v_ref, o_ref, vt_ref, *, group, qb, window, seq_len, unroll):
    tq = o_ref.shape[0]
    n_sub = tq // qb
    span = qb + 2 * window
    h = pl.program_id(1)
    i = pl.program_id(2)
    vt_chunk = min(512, seq_len)

    @pl.when(i == 0)
    def _():
        def tbody(c, carry):
            start = pl.multiple_of(c * vt_chunk, vt_chunk)
            vt_ref[:, pl.ds(start, vt_chunk)] = v_ref[pl.ds(start, vt_chunk), :].T
            return carry
        lax.fori_loop(0, seq_len // vt_chunk, tbody, 0)

    sink_row = jnp.concatenate(
        [jnp.full((1, qb), sink_ref[h * group + g] * LOG2E, F32) for g in range(group)], axis=1)
    rel0 = (lax.broadcasted_iota(jnp.int32, (span, qb), 0) - lax.broadcasted_iota(jnp.int32, (span, qb), 1))

    def one_block(sb):
        q0 = pl.multiple_of(sb * qb, qb)
        q_abs = i * tq + q0
        k0 = pl.multiple_of(jnp.clip(q_abs - window, 0, seq_len - span), LANES)
        rel = rel0 + (k0 - q_abs)
        bias = jnp.where(jnp.abs(rel) <= window, 0.0, NEG).astype(F32)
        q = jnp.concatenate([q_ref[pl.ds(q0, qb), g * LANES:(g + 1) * LANES] for g in range(group)], axis=0)
        k = k_ref[pl.ds(k0, span), :]
        st = lax.dot_general(k, q, (((1,), (1,)), ((), ())), preferred_element_type=F32)
        st = st + jnp.concatenate([bias] * group, axis=1)
        m = jnp.maximum(jnp.max(st, axis=0, keepdims=True), sink_row)
        pt = jnp.exp2(st - m)
        denom = jnp.sum(pt, axis=0, keepdims=True) + jnp.exp2(sink_row - m)
        out_t = jnp.dot(vt_ref[:, pl.ds(k0, span)], pt.astype(BF16), preferred_element_type=F32) / denom
        out = out_t.T
        for g in range(group):
            o_ref[pl.ds(q0, qb), g * LANES:(g + 1) * LANES] = out[g * qb:(g + 1) * qb, :].astype(o_ref.dtype)

    def body(sbb, carry):
        for u in range(unroll):
            one_block(sbb * unroll + u)
        return carry

    lax.fori_loop(0, n_sub // unroll, body, 0)


def _banded_attention(q, k, v, sink, n_seq, seq_len, name):
    n_tok = n_seq * seq_len
    group = B_HEADS // B_KV_HEADS
    gw = group * LANES
    qb, window = 128, B_WINDOW
    tq = min(1024, seq_len)
    q_blocks = seq_len // tq
    q_spec = pl.BlockSpec((tq, gw), lambda b, h, i: (b * q_blocks + i, h))
    kv_spec = pl.BlockSpec((seq_len, LANES), lambda b, h, i: (b, h))
    span = qb + 2 * window
    unroll = min(4, tq // qb)
    assert seq_len >= span and (tq // qb) % unroll == 0
    est = 4 * tq * gw * 2 + 5 * seq_len * LANES * 2 + 6 * unroll * group * qb * span * 4
    kern = functools.partial(_banded_attn_kernel, group=group, qb=qb, window=window, seq_len=seq_len,
                             unroll=unroll)
    return pl.pallas_call(
        kern,
        grid=(n_seq, B_KV_HEADS, q_blocks),
        in_specs=[pl.BlockSpec(memory_space=pltpu.SMEM), q_spec, kv_spec, kv_spec],
        out_specs=q_spec,
        out_shape=jax.ShapeDtypeStruct((n_tok, B_HEADS * LANES), BF16),
        scratch_shapes=[pltpu.VMEM((LANES, seq_len), BF16)],
        compiler_params=pltpu.CompilerParams(dimension_semantics=("parallel", "parallel", "arbitrary"),
                                             vmem_limit_bytes=_vmem_limit(est)),
        name=name,
    )(sink.astype(F32), q, k, v)


def _out_kernel(*refs, final):
    if final:
        x_ref, o_ref, g_ref, w_ref, fn_ref, y_ref = refs
    else:
        x_ref, o_ref, g_ref, w_ref, y_ref = refs
    u = (o_ref[...].astype(F32) * g_ref[...].astype(F32)).astype(BF16)
    y = x_ref[...] + jnp.dot(u, w_ref[...], preferred_element_type=F32)
    if final:
        ms = jnp.mean(y * y, axis=-1, keepdims=True)
        y = y * lax.rsqrt(ms + EPS) * fn_ref[...]
    y_ref[...] = y


def _output(x, o, gate_arr, gate_col_block, w_o, final_norm=None, name="out"):
    n_tok, d = x.shape
    width = o.shape[1]
    tm = min(512, n_tok)
    final = final_norm is not None
    in_specs = [
        pl.BlockSpec((tm, d), lambda i: (i, 0)),
        pl.BlockSpec((tm, width), lambda i: (i, 0)),
        pl.BlockSpec((tm, width), lambda i: (i, gate_col_block)),
        pl.BlockSpec((width, d), lambda i: (0, 0), pipeline_mode=pl.Buffered(1)),
    ]
    args = [x, o, gate_arr, w_o]
    if final:
        in_specs.append(pl.BlockSpec((1, d), lambda i: (0, 0)))
        args.append(final_norm.reshape(1, d).astype(F32))
    est = (width * d * 2 + 4 * tm * d * 4 + 4 * tm * width * 2 + 3 * tm * width * 4 + 2 * tm * d * 4)
    return pl.pallas_call(
        functools.partial(_out_kernel, final=final),
        grid=(n_tok // tm,),
        in_specs=in_specs,
        out_specs=pl.BlockSpec((tm, d), lambda i: (i, 0)),
        out_shape=jax.ShapeDtypeStruct((n_tok, d), F32),
        compiler_params=pltpu.CompilerParams(dimension_semantics=("parallel",),
                                             vmem_limit_bytes=_vmem_limit(est)),
        name=name,
    )(*args)


def _angles(pos, dim, theta):
    inv = np.float32(theta) ** (-np.arange(dim // 2, dtype=np.float32) * np.float32(2.0 / dim))
    ang = pos.astype(np.float32)[:, None] * inv[None, :]
    return jnp.asarray(np.cos(ang), F32), jnp.asarray(np.sin(ang), F32)


def _tables_a(seq_len, scale):
    c, s = _angles(np.arange(seq_len), A_ROPE, A_THETA)
    z = jnp.zeros_like(c)
    cos = jnp.concatenate([c, z, c, z], axis=1)
    sin_signed = jnp.concatenate([-s, z, s, z], axis=1)
    return cos * scale, sin_signed * scale


def _tables_b(seq_len, scale):
    c, s = _angles(np.arange(seq_len), B_ROT, B_THETA)
    z16, one, z96 = jnp.zeros_like(c), jnp.ones((seq_len, 96), F32), jnp.zeros((seq_len, 96), F32)
    cos = jnp.concatenate([c, c, one], axis=1)
    s1 = jnp.concatenate([z16, s, z96], axis=1)
    s2 = jnp.concatenate([-s, z16, z96], axis=1)
    return cos * scale, s1 * scale, s2 * scale


def _tables_c(seq_len):
    t = np.arange(seq_len)
    half = C_HEAD_DIM // 2
    cr, sr = _angles(t // GRID_W, half, C_THETA)
    cc, sc = _angles(t % GRID_W, half, C_THETA)
    z = jnp.zeros_like(cr)
    cos = jnp.concatenate([cr, cr, cc, cc], axis=1)
    s1 = jnp.concatenate([z, sr, z, sc], axis=1)
    s2 = jnp.concatenate([-sr, z, -sc, z], axis=1)
    return cos, s1, s2


def _layer_a(x, n_seq, seq_len, norm_w, w_in, q_norm, w_uq, kv_norm, w_ukv, w_o, final_norm, tag):
    d = x.shape[1]
    width = A_HEADS * A_V
    r0 = A_Q_RANK + A_KV_RANK
    half = A_ROPE // 2
    zpad = jnp.zeros((d, LANES // 2 - half), w_in.dtype)
    w_kr = jnp.concatenate([w_in[:, r0:r0 + half], zpad, w_in[:, r0 + half:r0 + A_ROPE], zpad], axis=1)
    ws = [w_in[:, r0 + A_ROPE:].astype(BF16), w_in[:, :r0].astype(BF16), w_kr.astype(BF16)]
    n_plain = width + r0
    segs = [Seg("silu", 0, width, 0, 0, w_idx=0), Seg("plain", 0, r0, 0, width, w_idx=1),
            Seg("rope", 0, LANES, 1, 0, w_idx=2)]
    main, k_rope = _projection(x, 0, d, norm_w, ws, segs, [n_plain, LANES], seq_len,
                               tables=_tables_a(seq_len, 1.0), roll_by=A_ROPE // 2, name=f"a_in_{tag}")
    cq_block, ckv_block = width // A_Q_RANK, (width + A_Q_RANK) // A_KV_RANK

    scale = (A_NOPE + A_ROPE) ** -0.5 * LOG2E
    wq = w_uq.reshape(A_Q_RANK, A_HEADS, A_NOPE + A_ROPE)
    zq = jnp.zeros((A_Q_RANK, A_HEADS, LANES // 2 - half), wq.dtype)
    wq_rope = jnp.concatenate([wq[:, :, A_NOPE:A_NOPE + half], zq, wq[:, :, A_NOPE + half:], zq], axis=2)
    ws = [wq[:, :, :A_NOPE].reshape(A_Q_RANK, -1).astype(BF16), wq_rope.reshape(A_Q_RANK, -1).astype(BF16)]
    hn = A_HEADS * LANES
    segs = [Seg("plain", 0, hn, 0, 0, scale=scale, w_idx=0), Seg("rope", 0, hn, 1, 0, w_idx=1)]
    q_nope, q_rope = _projection(main, cq_block, A_Q_RANK, q_norm, ws, segs, [hn, hn], seq_len,
                                 tables=_tables_a(seq_len, scale), roll_by=A_ROPE // 2, name=f"a_q_{tag}")

    wkv = w_ukv.reshape(A_KV_RANK, A_HEADS, A_NOPE + A_V)
    ws = [wkv[:, :, :A_NOPE].reshape(A_KV_RANK, -1).astype(BF16), wkv[:, :, A_NOPE:].reshape(A_KV_RANK, -1).astype(BF16)]
    segs = [Seg("plain", 0, hn, 0, 0, w_idx=0), Seg("plain", 0, hn, 1, 0, w_idx=1)]
    k_nope, v = _projection(main, ckv_block, A_KV_RANK, kv_norm, ws, segs, [hn, hn], seq_len, name=f"a_kv_{tag}")

    o = _dense_attention([q_nope, q_rope], [k_nope, k_rope], v, n_seq, seq_len, A_HEADS, 1, name=f"a_attn_{tag}")
    return _output(x, o, main, 0, w_o.astype(BF16), final_norm, name=f"a_out_{tag}")


def _gqa_projection(x, seq_len, norm_w, w_in, heads, kv_heads, kind, tables, gains, q_scale, roll_by, name):
    d = x.shape[1]
    qw, kw = heads * LANES, kv_heads * LANES
    segs = [Seg(kind, 0, qw, 0, 0, scale=q_scale, gain_idx=0), Seg(kind, qw, kw, 1, 0, gain_idx=1),
            Seg("plain", qw + kw, kw, 2, 0), Seg("silu", qw + 2 * kw, qw, 3, 0)]
    return _projection(x, 0, d, norm_w, [w_in.astype(BF16)], segs, [qw, kw, kw, qw], seq_len,
                       tables=tables, gains=gains, roll_by=roll_by, name=name)


def _layer_b(x, n_seq, seq_len, norm_w, w_in, sink, w_o, final_norm, tag):
    scale = B_HEAD_DIM ** -0.5 * LOG2E
    tq = _tables_b(seq_len, 1.0)
    q, k, v, gate = _gqa_projection(x, seq_len, norm_w, w_in, B_HEADS, B_KV_HEADS, "rope", tq, (), scale,
                                    B_ROT // 2, f"b_in_{tag}")
    o = _banded_attention(q, k, v, sink, n_seq, seq_len, name=f"b_attn_{tag}")
    return _output(x, o, gate, 0, w_o.astype(BF16), final_norm, name=f"b_out_{tag}")


def _layer_c(x, n_seq, seq_len, norm_w, w_in, q_gain, k_gain, w_o, final_norm, tag):
    scale = C_HEAD_DIM ** -0.5 * LOG2E
    gains = (q_gain.reshape(1, LANES).astype(F32), k_gain.reshape(1, LANES).astype(F32))
    q, k, v, gate = _gqa_projection(x, seq_len, norm_w, w_in, C_HEADS, C_KV_HEADS, "hnorm_rope",
                                    _tables_c(seq_len), gains, scale, C_HEAD_DIM // 4, f"c_in_{tag}")
    o = _dense_attention([q], [k], v, n_seq, seq_len, C_KV_HEADS, C_HEADS // C_KV_HEADS, name=f"c_attn_{tag}")
    return _output(x, o, gate, 0, w_o.astype(BF16), final_norm, name=f"c_out_{tag}")


def _trunk(x3, tag, norm_w, a_w_in, a_q_norm, a_w_uq, a_kv_norm, a_w_ukv, a_w_o, b_w_in, b_sink, b_w_o,
           c_w_in, c_q_scale, c_k_scale, c_w_o, final_norm):
    n_seq, seq_len, d = x3.shape
    depth = norm_w.shape[0]
    x = x3.reshape(n_seq * seq_len, d)
    for i in range(depth):
        kind, j = i % 3, i // 3
        fn = final_norm if i == depth - 1 else None
        if kind == 0:
            x = _layer_a(x, n_seq, seq_len, norm_w[i], a_w_in[j], a_q_norm[j], a_w_uq[j], a_kv_norm[j],
                         a_w_ukv[j], a_w_o[j], fn, f"{tag}{i}")
        elif kind == 1:
            x = _layer_b(x, n_seq, seq_len, norm_w[i], b_w_in[j], b_sink[j], b_w_o[j], fn, f"{tag}{i}")
        else:
            x = _layer_c(x, n_seq, seq_len, norm_w[i], c_w_in[j], c_q_scale[j], c_k_scale[j], c_w_o[j], fn,
                         f"{tag}{i}")
    return x.reshape(n_seq, seq_len, d)


def kernel(x_prompt, x_sample, norm_w, a_w_in, a_q_norm, a_w_uq, a_kv_norm, a_w_ukv, a_w_o, b_w_in, b_sink,
           b_w_o, c_w_in, c_q_scale, c_k_scale, c_w_o, final_norm):
    params = (norm_w, a_w_in, a_q_norm, a_w_uq, a_kv_norm, a_w_ukv, a_w_o, b_w_in, b_sink, b_w_o,
              c_w_in, c_q_scale, c_k_scale, c_w_o, final_norm)
    return (_trunk(x_prompt, "p", *params), _trunk(x_sample, "s", *params))
```
